```python
import jax, jax.numpy as jnp
from jax import lax
import numpy as np


D_MODEL = 1024
BATCH = 4
SEQ = 8192
DEPTH = 2

D_MIX = D_MODEL
RET_HEADS = 4
RET_DK = D_MIX // 16
RET_DV = D_MIX // 16
HG_HEADS = 4
HG_DK = D_MIX // 16
HG_DV = D_MIX // 16
GDN_HEADS = 4
GDN_DK = D_MIX // 8
GDN_DV = D_MIX // 8
GDN_CONV = 4
FFN_CONV = 3
D_FF = ((8 * D_MODEL // 3 + 127) // 128) * 128
RET_CHUNK = 64
HG_CHUNK = 16
GDN_CHUNK = 64
ROPE_BASE = 10000.0
NORM_EPS = 1e-6
EXP_CLIP = 80.0
PROJ_SPLITS = (
    RET_HEADS * RET_DK, RET_HEADS * RET_DK, RET_HEADS * RET_DV, RET_HEADS * RET_DV,
    HG_HEADS * HG_DK, HG_HEADS * HG_DK, HG_HEADS * HG_DV, HG_HEADS * HG_DV,
    GDN_HEADS * GDN_DK, GDN_HEADS * GDN_DK, GDN_HEADS * GDN_DV, GDN_HEADS, GDN_HEADS, GDN_HEADS * GDN_DV,
)
D_PROJ = sum(PROJ_SPLITS)

kernel_name = 'hybrid_retention_hgrn2_gdn_convffn'


def rmsnorm(x, w=None):
    xf = x.astype(jnp.float32)
    y = xf * lax.rsqrt(jnp.mean(xf * xf, axis=-1, keepdims=True) + NORM_EPS)
    if w is not None:
        y = y * w.astype(jnp.float32)
    return y.astype(x.dtype)


def l2norm(x):
    return x * lax.rsqrt(jnp.sum(x * x, axis=-1, keepdims=True) + NORM_EPS)


def causal_dwconv(x, w):
    K, C = w.shape
    return lax.conv_general_dilated(
        x, w[:, None, :].astype(x.dtype), window_strides=(1,), padding=[(K - 1, 0)],
        dimension_numbers=('NWC', 'WIO', 'NWC'), feature_group_count=C)


def rotary(x, positions):
    half = x.shape[-1] // 2
    inv_freq = ROPE_BASE ** (-jnp.arange(half, dtype=jnp.float32) / half)
    ang = positions.astype(jnp.float32)[..., None] * inv_freq
    cos, sin = jnp.cos(ang)[:, :, None, :], jnp.sin(ang)[:, :, None, :]
    x1, x2 = x[..., :half], x[..., half:]
    return jnp.concatenate([x1 * cos - x2 * sin, x1 * sin + x2 * cos], axis=-1)


def _chunks(x, C):
    B, H, T = x.shape[:3]
    return x.reshape((B, H, T // C, C) + x.shape[3:])


def masked_exp(diff, mask):
    return jnp.where(mask, jnp.exp(jnp.where(mask, diff, 0.0)), 0.0)


def diag_state_scan(U, w):
    def step(S, inp):
        U_n, w_n = inp
        return w_n * S + U_n, S
    S0 = jnp.zeros(U.shape[:2] + U.shape[3:], jnp.float32)
    _, S_prev = lax.scan(step, S0, (jnp.moveaxis(U, 2, 0), jnp.moveaxis(w, 2, 0)))
    return jnp.moveaxis(S_prev, 0, 2)


def retention_chunk(q, k, v, log_gamma):
    C = RET_CHUNK
    B, H, T, _ = q.shape
    qc, kc, vc = _chunks(q, C), _chunks(k, C), _chunks(v, C)
    idx = jnp.arange(C, dtype=jnp.float32)
    rel = idx[:, None] - idx[None, :]
    causal = rel >= 0
    lg = log_gamma[:, None, None]
    D = masked_exp(rel[None] * lg, causal[None])
    scores = jnp.einsum('bhnid,bhnjd->bhnij', qc, kc) * D[None, :, None]
    intra = jnp.einsum('bhnij,bhnje->bhnie', scores, vc)
    lg1 = log_gamma[:, None]
    q_in = qc * jnp.exp((idx + 1.0)[None] * lg1)[None, :, None, :, None]
    k_out = kc * jnp.exp((C - 1.0 - idx)[None] * lg1)[None, :, None, :, None]
    U = jnp.einsum('bhnjd,bhnje->bhnde', k_out, vc)
    N = qc.shape[2]
    w = jnp.broadcast_to(jnp.exp(C * log_gamma)[None, :, None, None, None], (1, H, N, 1, 1))
    S_prev = diag_state_scan(U, w)
    inter = jnp.einsum('bhnid,bhnde->bhnie', q_in, S_prev)
    return (intra + inter).reshape(B, H, T, -1)


def gla_chunk(q, k, v, g):
    C = HG_CHUNK
    B, H, T, _ = q.shape
    qc, kc, vc = _chunks(q, C), _chunks(k, C), _chunks(v, C)
    b = jnp.cumsum(_chunks(g, C), axis=3)
    idx = jnp.arange(C)
    causal = (idx[:, None] >= idx[None, :])[:, :, None]
    decay = masked_exp(b[:, :, :, :, None, :] - b[:, :, :, None, :, :], causal)
    scores = jnp.einsum('bhnid,bhnjd,bhnijd->bhnij', qc, kc, decay)
    intra = jnp.einsum('bhnij,bhnje->bhnie', scores, vc)
    b_last = b[:, :, :, -1:, :]
    q_in = qc * jnp.exp(b)
    k_out = kc * jnp.exp(b_last - b)
    U = jnp.einsum('bhnjd,bhnje->bhnde', k_out, vc)
    w = jnp.exp(b_last[:, :, :, 0, :])[..., None]
    S_prev = diag_state_scan(U, w)
    inter = jnp.einsum('bhnid,bhnde->bhnie', q_in, S_prev)
    return (intra + inter).reshape(B, H, T, -1)


def gated_delta_chunk(q, k, v, g, beta):
    C = GDN_CHUNK
    B, H, T, dk = q.shape
    dv = v.shape[-1]
    qc, kc, vc = _chunks(q, C), _chunks(k, C), _chunks(v, C)
    b = jnp.cumsum(_chunks(g, C), axis=-1)
    bc = _chunks(beta, C)[..., None]
    idx = jnp.arange(C)
    causal = idx[:, None] >= idx[None, :]
    strict = idx[:, None] > idx[None, :]
    decay = masked_exp(b[..., :, None] - b[..., None, :], causal)
    kb = kc * bc
    L = jnp.where(strict, jnp.einsum('bhnid,bhnjd->bhnij', kb, kc) * decay, 0.0)
    rhs = jnp.concatenate([vc * bc, kb * jnp.exp(b)[..., None]], axis=-1)
    sol = lax.linalg.triangular_solve(L, rhs, left_side=True, lower=True, unit_diagonal=True)
    u, kcd = sol[..., :dv], sol[..., dv:]
    attn = jnp.einsum('bhnid,bhnjd->bhnij', qc, kc) * decay
    q_in = qc * jnp.exp(b)[..., None]
    k_out = kc * jnp.exp(b[..., -1:] - b)[..., None]
    w = jnp.exp(b[..., -1])

    def step(S, inp):
        u_n, kcd_n, q_n, k_n, a_n, w_n = inp
        v_new = u_n - jnp.einsum('bhcd,bhde->bhce', kcd_n, S)
        o_n = jnp.einsum('bhcd,bhde->bhce', q_n, S) + jnp.einsum('bhij,bhje->bhie', a_n, v_new)
        S = S * w_n[..., None, None] + jnp.einsum('bhcd,bhce->bhde', k_n, v_new)
        return S, o_n

    xs = tuple(jnp.moveaxis(t, 2, 0) for t in (u, kcd, q_in, k_out, attn, w))
    S0 = jnp.zeros((B, H, dk, dv), jnp.float32)
    _, o = lax.scan(step, S0, xs)
    return jnp.moveaxis(o, 0, 2).reshape(B, H, T, dv)


def hybrid_mixer(h, positions, w_in, lb, conv_w, A_log, dt_bias, hg_norm_w, gdn_norm_w, w_out):
    B, T, _ = h.shape
    f32 = jnp.float32
    proj = h @ w_in
    split_at = np.cumsum(PROJ_SPLITS)[:-1].tolist()
    (rq, rk, rv, rg, hq, hf, hi, hg, gq, gk, gv, ga, gb, gg) = jnp.split(proj, split_at, axis=-1)

    def heads(t, H):
        return t.astype(f32).reshape(B, T, H, -1)

    def bhtd(t):
        return jnp.transpose(t, (0, 2, 1, 3))

    rq_h = rotary(heads(rq, RET_HEADS), positions)
    rk_h = rotary(heads(rk, RET_HEADS), positions) * RET_DK ** -0.5
    log_gamma = jnp.log1p(-jnp.exp2(-5.0 - jnp.arange(RET_HEADS, dtype=f32)))
    o = retention_chunk(bhtd(rq_h), bhtd(rk_h), bhtd(heads(rv, RET_HEADS)), log_gamma)
    o_ret = (rmsnorm(bhtd(o)) * jax.nn.silu(heads(rg, RET_HEADS))).reshape(B, T, -1)

    z = heads(hf, HG_HEADS)
    lb_h = lb.astype(f32).reshape(HG_HEADS, HG_DK)
    log_f = jax.nn.log_sigmoid(z) + jnp.log1p(lb_h * jnp.exp(jnp.minimum(-z, EXP_CLIP)))
    k_in = (1.0 - lb_h) * jax.nn.sigmoid(-z)
    o = gla_chunk(bhtd(jax.nn.silu(heads(hq, HG_HEADS))), bhtd(k_in), bhtd(heads(hi, HG_HEADS)), bhtd(log_f))
    o_hg = (rmsnorm(bhtd(o), hg_norm_w) * jax.nn.silu(heads(hg, HG_HEADS))).reshape(B, T, -1)

    qkv = jax.nn.silu(causal_dwconv(jnp.concatenate([gq, gk, gv], axis=-1), conv_w).astype(f32))
    cq, ck, cv = jnp.split(qkv, [GDN_HEADS * GDN_DK, 2 * GDN_HEADS * GDN_DK], axis=-1)
    q = l2norm(cq.reshape(B, T, GDN_HEADS, GDN_DK)) * GDN_DK ** -0.5
    k = l2norm(ck.reshape(B, T, GDN_HEADS, GDN_DK))
    v = cv.reshape(B, T, GDN_HEADS, GDN_DV)
    g = -jnp.exp(A_log.astype(f32)) * jax.nn.softplus(ga.astype(f32) + dt_bias.astype(f32))
    beta = jax.nn.sigmoid(gb.astype(f32))
    o = gated_delta_chunk(bhtd(q), bhtd(k), bhtd(v), jnp.transpose(g, (0, 2, 1)), jnp.transpose(beta, (0, 2, 1)))
    o_gdn = (rmsnorm(bhtd(o), gdn_norm_w) * jax.nn.silu(heads(gg, GDN_HEADS))).reshape(B, T, -1)

    mixed = jnp.concatenate([o_ret, o_hg, o_gdn], axis=-1).astype(h.dtype)
    return mixed @ w_out


def conv_ffn(h, w_up, conv_w, conv_b, w_down):
    u = causal_dwconv(h @ w_up, conv_w) + conv_b
    gate, val = jnp.split(u, 2, axis=-1)
    return (jax.nn.silu(gate) * val) @ w_down


def setup_inputs(seed: int = 0) -> dict:
    key = jax.random.key(seed)
    ks = jax.random.split(key, 18)
    f32 = jnp.float32
    nrm = lambda k, s: jax.random.normal(k, s, f32)
    dt = jnp.exp(jax.random.uniform(ks[6], (DEPTH, GDN_HEADS), f32, np.log(1e-3), np.log(1e-1)))
    return {
        'x': nrm(ks[0], (BATCH, SEQ, D_MODEL)),
        'positions': jnp.broadcast_to(jnp.arange(SEQ, dtype=jnp.int32), (BATCH, SEQ)),
        'norm1_w': 1.0 + 0.02 * nrm(ks[1], (DEPTH, D_MODEL)),
        'w_in': nrm(ks[2], (DEPTH, D_MODEL, D_PROJ)) * D_MODEL ** -0.5,
        'hg_lb_logits': 0.5 * nrm(ks[3], (DEPTH, HG_HEADS * HG_DK)),
        'gdn_conv_w': nrm(ks[4], (DEPTH, GDN_CONV, GDN_HEADS * (2 * GDN_DK + GDN_DV))) * GDN_CONV ** -0.5,
        'gdn_A_log': jnp.log(jax.random.uniform(ks[5], (DEPTH, GDN_HEADS), f32, 1.0, 16.0)),
        'gdn_dt_bias': dt + jnp.log(-jnp.expm1(-dt)),
        'hg_norm_w': 1.0 + 0.02 * nrm(ks[7], (DEPTH, HG_DV)),
        'gdn_norm_w': 1.0 + 0.02 * nrm(ks[8], (DEPTH, GDN_DV)),
        'w_out': nrm(ks[9], (DEPTH, D_MIX, D_MODEL)) * D_MIX ** -0.5,
        'norm2_w': 1.0 + 0.02 * nrm(ks[10], (DEPTH, D_MODEL)),
        'w_up': nrm(ks[11], (DEPTH, D_MODEL, 2 * D_FF)) * D_MODEL ** -0.5,
        'ffn_conv_w': nrm(ks[12], (DEPTH, FFN_CONV, 2 * D_FF)) * FFN_CONV ** -0.5,
        'ffn_conv_b': 0.01 * nrm(ks[13], (DEPTH, 2 * D_FF)),
        'w_down': nrm(ks[14], (DEPTH, D_FF, D_MODEL)) * D_FF ** -0.5,
        'final_norm_w': 1.0 + 0.02 * nrm(ks[15], (D_MODEL,)),
    }


def reference(x, positions, norm1_w, w_in, hg_lb_logits, gdn_conv_w, gdn_A_log, gdn_dt_bias,
              hg_norm_w, gdn_norm_w, w_out, norm2_w, w_up, ffn_conv_w, ffn_conv_b, w_down,
              final_norm_w):
    p = jax.nn.softmax(hg_lb_logits.astype(jnp.float32), axis=0)
    lower_bounds = jnp.maximum(jnp.cumsum(p, axis=0) - p[0:1], 0.0)
    for layer in range(DEPTH):
        h = rmsnorm(x, norm1_w[layer])
        x = x + hybrid_mixer(h, positions, w_in[layer], lower_bounds[layer], gdn_conv_w[layer],
                             gdn_A_log[layer], gdn_dt_bias[layer], hg_norm_w[layer],
                             gdn_norm_w[layer], w_out[layer])
        h = rmsnorm(x, norm2_w[layer])
        x = x + conv_ffn(h, w_up[layer], ffn_conv_w[layer], ffn_conv_b[layer], w_down[layer])
    return rmsnorm(x, final_norm_w)
```

```python
import functools

import numpy as np
import jax
import jax.numpy as jnp
from jax import lax
from jax.experimental import pallas as pl
from jax.experimental.pallas import tpu as pltpu

F32 = jnp.float32
BF16 = jnp.bfloat16

D_MODEL = 1024
RET_HEADS = 4
RET_D = 64
HG_HEADS = 4
HG_D = 64
GDN_HEADS = 4
GDN_D = 128
GDN_CONV = 4
FFN_CONV = 3
D_FF = 2816
ROPE_BASE = 10000.0
NORM_EPS = 1e-6
EXP_CLIP = 80.0

LANES = 128
SUBLANES = 8
TB = 256
HG_CHUNK = 16
GDN_CHUNK = 64
FF_TILE = 256
D_PROJ_PAD = 4224
VMEM_LIMIT = 56 * 1024 * 1024

_RQ, _RK, _RV, _RG = 0, 256, 512, 768
_HQ, _HF, _HI, _HG = 1024, 1280, 1536, 1792
_GQKV, _GG, _GAB = 2048, 3584, 4096


def _dot(a, b):
    return jnp.dot(a.astype(BF16), b.astype(BF16), preferred_element_type=F32)


def _dot_nt(a, b):
    return lax.dot_general(a.astype(BF16), b.astype(BF16), (((1,), (1,)), ((), ())),
                           preferred_element_type=F32)


def _dot_tn(a, b):
    return lax.dot_general(a.astype(BF16), b.astype(BF16), (((0,), (0,)), ((), ())),
                           preferred_element_type=F32)


def _dot_f32(a, b):
    return jnp.dot(a, b, preferred_element_type=F32, precision=lax.Precision.HIGHEST)


def _dot_nt_f32(a, b):
    return lax.dot_general(a, b, (((1,), (1,)), ((), ())), preferred_element_type=F32,
                           precision=lax.Precision.HIGHEST)


def _sigmoid(x):
    return 1.0 / (1.0 + jnp.exp(-x))


def _silu(x):
    return x * _sigmoid(x)


def _softplus(x):
    return jnp.maximum(x, 0.0) + jnp.log1p(jnp.exp(-jnp.abs(x)))


def _chunk_cumsum(x, chunk):
    row = lax.broadcasted_iota(jnp.int32, (x.shape[0], 1), 0) % chunk
    s = 1
    while s < chunk:
        x = x + jnp.where(row >= s, pltpu.roll(x, s, axis=0), 0.0)
        s *= 2
    return x


def _causal_conv(u, prev, w, width):
    acc = u * w[width - 1:width]
    row = lax.broadcasted_iota(jnp.int32, (SUBLANES, 1), 0)
    for s in range(1, width):
        rolled = pltpu.roll(u, s, axis=0)
        head = jnp.where(row < s, pltpu.roll(prev, s, axis=0), rolled[0:SUBLANES])
        shifted = jnp.concatenate([head, rolled[SUBLANES:]], axis=0)
        acc = acc + shifted * w[width - 1 - s:width - s]
    return acc


def _rope_kernel(pos_ref, invf_ref, cos_ref, sin_ref):
    ang = pos_ref[0] * invf_ref[...]
    cos_ref[0] = jnp.cos(ang)
    sin_ref[0] = jnp.sin(ang)


def _rope_tables(positions):
    B, T = positions.shape
    half = RET_D // 2
    inv_freq = ROPE_BASE ** (-jnp.arange(half, dtype=F32) / half)
    invf = jnp.tile(inv_freq, LANES // half)[None, :]
    pos = positions.astype(F32)[..., None]
    tb = 512 if T % 512 == 0 else TB
    return pl.pallas_call(
        _rope_kernel,
        grid=(B, T // tb),
        in_specs=[pl.BlockSpec((1, tb, 1), lambda b, t: (b, t, 0)),
                  pl.BlockSpec((1, LANES), lambda b, t: (0, 0))],
        out_specs=[pl.BlockSpec((1, tb, LANES), lambda b, t: (b, t, 0)),
                   pl.BlockSpec((1, tb, LANES), lambda b, t: (b, t, 0))],
        out_shape=[jax.ShapeDtypeStruct((B, T, LANES), F32)] * 2,
        name="rope_tables",
    )(pos, invf)


def _retention(proj, cos, sin, decay_ref, din_ref, dout_ref, wdec_ref, bd_ref, qmask_ref,
               vmask_ref, ones64_ref, s_scr):
    q = proj[:, _RQ:_RQ + 256]
    k = proj[:, _RK:_RK + 256]
    v = proj[:, _RV:_RV + 256]
    g = proj[:, _RG:_RG + 256]

    def rot(a):
        a1, a2 = a[:, :LANES], a[:, LANES:]
        return jnp.concatenate([a1 * cos - a2 * sin, a1 * sin + a2 * cos], axis=1)

    qr = rot(q)
    kr = rot(k) * (RET_D ** -0.5)
    kr_b = kr.astype(BF16)
    intra = jnp.zeros_like(v)
    for h in range(RET_HEADS):
        a = _dot_nt(qr * qmask_ref[h], kr_b) * decay_ref[h]
        intra = intra + _dot(a, v * vmask_ref[h])
    s = s_scr[...]
    inter = _dot(qr * din_ref[...], s)
    u = _dot_tn(kr * dout_ref[...], v)
    s_scr[...] = wdec_ref[...] * s + bd_ref[...] * u
    o = intra + inter
    ss = _dot(o * o, ones64_ref[...])
    return o * lax.rsqrt(ss * (1.0 / RET_D) + NORM_EPS) * _silu(g)


def _hgrn2(proj_scr, lb, normw, ones64_ref, st_scr, q_scr, k_scr, b_scr, o_scr):
    hq = proj_scr[:, _HQ:_HQ + 256]
    z = proj_scr[:, _HF:_HF + 256]
    gate = proj_scr[:, _HG:_HG + 256]
    log_sig = jnp.minimum(z, 0.0) - jnp.log1p(jnp.exp(-jnp.abs(z)))
    log_f = log_sig + jnp.log1p(lb * jnp.exp(jnp.minimum(-z, EXP_CLIP)))
    q_scr[...] = _silu(hq)
    k_scr[...] = (1.0 - lb) * _sigmoid(-z)
    b_scr[...] = _chunk_cumsum(log_f, HG_CHUNK)
    ones64 = ones64_ref[...]
    ones64_f = ones64.astype(F32)
    C = HG_CHUNK
    rowi = lax.broadcasted_iota(jnp.int32, (C, 1), 0)

    def body(c, carry):
        r0 = pl.multiple_of(c * C, C)
        qc = q_scr[pl.ds(r0, C), :]
        kc = k_scr[pl.ds(r0, C), :]
        vc = proj_scr[pl.ds(r0, C), _HI:_HI + 256]
        bc = b_scr[pl.ds(r0, C), :]
        blast = bc[C - 1:C, :]
        st = st_scr[...]
        inter = _dot_nt(qc * jnp.exp(bc), st)
        u = _dot_tn(vc, kc * jnp.exp(blast - bc))
        st_scr[...] = jnp.exp(blast) * st + ones64_f * u
        parts = []
        for j in range(C):
            m = rowi >= j
            e = jnp.where(m, jnp.exp(jnp.where(m, bc - bc[j:j + 1, :], 0.0)), 0.0)
            parts.append((qc * kc[j:j + 1, :] * e).astype(BF16))
        r = jnp.dot(jnp.concatenate(parts, axis=0), ones64, preferred_element_type=F32)
        acc = inter
        for j in range(C):
            acc = acc + r[j * C:(j + 1) * C, :] * vc[j:j + 1, :]
        o_scr[pl.ds(r0, C), :] = acc
        return carry

    lax.fori_loop(0, TB // C, body, 0)
    o = o_scr[...]
    ss = _dot(o * o, ones64)
    return o * lax.rsqrt(ss * (1.0 / HG_D) + NORM_EPS) * normw * _silu(gate)


def _gdn(proj, convw, alog, dtb, normw, carry_scr, s_scr):
    raw = proj[:, _GQKV:_GQKV + 3 * GDN_HEADS * GDN_D]
    qkv = _silu(_causal_conv(raw, carry_scr[...], convw, GDN_CONV))
    carry_scr[...] = raw[TB - SUBLANES:TB]
    W = GDN_HEADS * GDN_D
    gab = proj[:, _GAB:_GAB + LANES]
    gate = proj[:, _GG:_GG + W]
    g_all = -jnp.exp(alog) * _softplus(gab + dtb)
    beta_all = _sigmoid(gab)
    b_all = _chunk_cumsum(g_all, GDN_CHUNK)
    b_rows = jnp.transpose(b_all)

    C = GDN_CHUNK
    ii = lax.broadcasted_iota(jnp.int32, (C, C), 0)
    jj = lax.broadcasted_iota(jnp.int32, (C, C), 1)
    causal = ii >= jj
    strict = ii > jj
    eye = jnp.where(ii == jj, 1.0, 0.0).astype(F32)

    outs = []
    for h in range(GDN_HEADS):
        lo = h * GDN_D
        qh = qkv[:, lo:lo + GDN_D]
        kh = qkv[:, W + lo:W + lo + GDN_D]
        vh = qkv[:, 2 * W + lo:2 * W + lo + GDN_D]
        qh = qh * lax.rsqrt(jnp.sum(qh * qh, axis=-1, keepdims=True) + NORM_EPS) * (GDN_D ** -0.5)
        kh = kh * lax.rsqrt(jnp.sum(kh * kh, axis=-1, keepdims=True) + NORM_EPS)
        chunks = []
        for n in range(TB // C):
            r0 = n * C
            qc, kc, vc = qh[r0:r0 + C], kh[r0:r0 + C], vh[r0:r0 + C]
            bcol = b_all[r0:r0 + C, h:h + 1]
            brow = b_rows[h:h + 1, r0:r0 + C]
            beta = beta_all[r0:r0 + C, GDN_HEADS + h:GDN_HEADS + h + 1]
            decay = jnp.where(causal, jnp.exp(jnp.where(causal, bcol - brow, 0.0)), 0.0)
            kb = kc * beta
            lmat = jnp.where(strict, _dot_nt_f32(kb, kc) * decay, 0.0)
            inv = eye - lmat
            pw = lmat
            step = 2
            while step < C:
                pw = _dot_f32(pw, pw)
                inv = inv + _dot_f32(inv, pw)
                step *= 2
            eb = jnp.exp(bcol)
            sol = _dot_f32(inv, jnp.concatenate([vc * beta, kb * eb], axis=1))
            u, kcd = sol[:, :GDN_D], sol[:, GDN_D:]
            attn = _dot_nt(qc, kc) * decay
            blast = bcol[C - 1:C, :]
            s = s_scr[h]
            v_new = u - _dot(kcd, s)
            chunks.append(_dot(qc * eb, s) + _dot(attn, v_new))
            s_scr[h] = s * jnp.exp(blast) + _dot_tn(kc * jnp.exp(blast - bcol), v_new)
        o = jnp.concatenate(chunks, axis=0)
        o = o * lax.rsqrt(jnp.mean(o * o, axis=-1, keepdims=True) + NORM_EPS)
        outs.append(o)
    return jnp.concatenate(outs, axis=1) * normw * _silu(gate)


def _mixer_kernel(x_ref, cos_ref, sin_ref, n1w_ref, win_ref, lb_ref, convw_ref, alog_ref, dtb_ref,
                  hgw_ref, gdw_ref, wout_ref, rdecay_ref, rdin_ref, rdout_ref, rwdec_ref, rbd_ref,
                  qmask_ref, vmask_ref, ones64_ref,
                  out_ref,
                  proj_scr, sret_scr, shg_scr, sgd_scr, carry_scr, hq_scr, hk_scr, hb_scr, ho_scr):
    @pl.when(pl.program_id(1) == 0)
    def _():
        sret_scr[...] = jnp.zeros_like(sret_scr)
        shg_scr[...] = jnp.zeros_like(shg_scr)
        sgd_scr[...] = jnp.zeros_like(sgd_scr)
        carry_scr[...] = jnp.zeros_like(carry_scr)

    x = x_ref[0]
    h = x * lax.rsqrt(jnp.mean(x * x, axis=-1, keepdims=True) + NORM_EPS) * n1w_ref[...]
    proj_scr[...] = jnp.dot(h.astype(BF16), win_ref[...], preferred_element_type=F32)
    proj = proj_scr[...]

    o_ret = _retention(proj, cos_ref[0], sin_ref[0], rdecay_ref, rdin_ref, rdout_ref, rwdec_ref,
                       rbd_ref, qmask_ref, vmask_ref, ones64_ref, sret_scr)
    o_hg = _hgrn2(proj_scr, lb_ref[...], hgw_ref[...], ones64_ref, shg_scr, hq_scr, hk_scr, hb_scr,
                  ho_scr)
    o_gdn = _gdn(proj, convw_ref[...], alog_ref[...], dtb_ref[...], gdw_ref[...], carry_scr, sgd_scr)
    mixed = jnp.concatenate([o_ret, o_hg, o_gdn], axis=1).astype(BF16)
    out_ref[0] = x + jnp.dot(mixed, wout_ref[...], preferred_element_type=F32)


def _retention_tables():
    hh = np.arange(RET_HEADS, dtype=np.float32)
    log_gamma = jnp.log1p(-jnp.exp2(-5.0 - jnp.asarray(hh)))
    idx = jnp.arange(TB, dtype=F32)
    rel = idx[:, None] - idx[None, :]
    causal = rel >= 0
    decay = jnp.where(causal[None], jnp.exp(jnp.where(causal[None], rel[None] * log_gamma[:, None, None], 0.0)), 0.0)
    lane = np.arange(256)
    head_qk = (lane % LANES) // (RET_D // 2)
    head_v = lane // RET_D
    lg_lane = log_gamma[head_qk]
    din = jnp.exp((idx + 1.0)[:, None] * lg_lane[None, :])
    dout = jnp.exp((TB - 1.0 - idx)[:, None] * lg_lane[None, :])
    bd = (head_qk[:, None] == head_v[None, :]).astype(np.float32)
    wdec = jnp.exp(TB * lg_lane)[:, None] * bd
    qmask = np.stack([(head_qk == h) for h in range(RET_HEADS)]).astype(np.float32)[:, None, :]
    vmask = np.stack([(head_v == h) for h in range(RET_HEADS)]).astype(np.float32)[:, None, :]
    ones64 = (head_v[:, None] == head_v[None, :]).astype(np.float32)
    return (decay, din, dout, wdec, jnp.asarray(bd), jnp.asarray(qmask), jnp.asarray(vmask),
            jnp.asarray(ones64, dtype=BF16))


def _proj_column_order():
    rq = np.arange(256)
    half, hh, j = rq // LANES, (rq % LANES) // 32, rq % 32
    rot_perm = hh * RET_D + half * 32 + j
    order = np.concatenate([
        rot_perm, 256 + rot_perm, np.arange(512, 1024),
        np.arange(1024, 2048),
        np.arange(2048, 3584),
        np.arange(3592, 4104),
        np.arange(3584, 3592),
        np.full(D_PROJ_PAD - 4104, 4104),
    ])
    assert order.shape[0] == D_PROJ_PAD
    return order


def _const_spec(shape):
    nd = len(shape)
    return pl.BlockSpec(shape, lambda b, t: (0,) * nd)


def _mixer_layer(x, cos_t, sin_t, n1w, win, lb, convw, alog, dtb, hgw, gdw, wout, tables):
    B, T, D = x.shape
    consts = [n1w, win, lb, convw, alog, dtb, hgw, gdw, wout, *tables]
    tok = lambda w: pl.BlockSpec((1, TB, w), lambda b, t: (b, t, 0))
    return pl.pallas_call(
        _mixer_kernel,
        grid=(B, T // TB),
        in_specs=[tok(D), tok(LANES), tok(LANES)] + [_const_spec(c.shape) for c in consts],
        out_specs=tok(D),
        out_shape=jax.ShapeDtypeStruct((B, T, D), F32),
        scratch_shapes=[
            pltpu.VMEM((TB, D_PROJ_PAD), F32),
            pltpu.VMEM((256, 256), F32),
            pltpu.VMEM((256, 256), F32),
            pltpu.VMEM((GDN_HEADS, GDN_D, GDN_D), F32),
            pltpu.VMEM((SUBLANES, 3 * GDN_HEADS * GDN_D), F32),
            pltpu.VMEM((TB, 256), F32),
            pltpu.VMEM((TB, 256), F32),
            pltpu.VMEM((TB, 256), F32),
            pltpu.VMEM((TB, 256), F32),
        ],
        compiler_params=pltpu.CompilerParams(
            dimension_semantics=("parallel", "arbitrary"), vmem_limit_bytes=VMEM_LIMIT),
        name="mixer_layer",
    )(x, cos_t, sin_t, *consts)


def _ffn_kernel(x_ref, n2w_ref, wg_ref, wv_ref, cwg_ref, cwv_ref, cbg_ref, cbv_ref, wd_ref, fw_ref,
                out_ref, carry_scr, act_scr, *, final_norm):
    @pl.when(pl.program_id(1) == 0)
    def _():
        carry_scr[...] = jnp.zeros_like(carry_scr)

    x = x_ref[0]
    h = (x * lax.rsqrt(jnp.mean(x * x, axis=-1, keepdims=True) + NORM_EPS) * n2w_ref[...]).astype(BF16)
    for c in range(D_FF // FF_TILE):
        cols = slice(c * FF_TILE, (c + 1) * FF_TILE)
        ys = []
        for i, (w_ref, cw_ref, cb_ref) in enumerate(((wg_ref, cwg_ref, cbg_ref), (wv_ref, cwv_ref, cbv_ref))):
            u = jnp.dot(h, w_ref[:, cols], preferred_element_type=F32)
            ys.append(_causal_conv(u, carry_scr[i, :, cols], cw_ref[:, cols], FFN_CONV) + cb_ref[:, cols])
            carry_scr[i, :, cols] = u[TB - SUBLANES:TB]
        act_scr[:, cols] = (_silu(ys[0]) * ys[1]).astype(BF16)
    o = x + jnp.dot(act_scr[...], wd_ref[...], preferred_element_type=F32)
    if final_norm:
        o = o * lax.rsqrt(jnp.mean(o * o, axis=-1, keepdims=True) + NORM_EPS) * fw_ref[...]
    out_ref[0] = o


def _ffn_layer(x, n2w, wg, wv, cwg, cwv, cbg, cbv, wd, fw, final_norm):
    B, T, D = x.shape
    consts = [n2w, wg, wv, cwg, cwv, cbg, cbv, wd, fw]
    tok = pl.BlockSpec((1, TB, D), lambda b, t: (b, t, 0))
    return pl.pallas_call(
        functools.partial(_ffn_kernel, final_norm=final_norm),
        grid=(B, T // TB),
        in_specs=[tok] + [_const_spec(c.shape) for c in consts],
        out_specs=tok,
        out_shape=jax.ShapeDtypeStruct((B, T, D), F32),
        scratch_shapes=[pltpu.VMEM((2, SUBLANES, D_FF), F32), pltpu.VMEM((TB, D_FF), BF16)],
        compiler_params=pltpu.CompilerParams(
            dimension_semantics=("parallel", "arbitrary"), vmem_limit_bytes=VMEM_LIMIT),
        name="ffn_layer",
    )(x, *consts)


def kernel(x, positions, norm1_w, w_in, hg_lb_logits, gdn_conv_w, gdn_A_log, gdn_dt_bias, hg_norm_w,
           gdn_norm_w, w_out, norm2_w, w_up, ffn_conv_w, ffn_conv_b, w_down, final_norm_w):
    depth = w_in.shape[0]
    assert x.shape[1] % TB == 0 and x.shape[2] == D_MODEL
    p = jax.nn.softmax(hg_lb_logits.astype(F32), axis=0)
    lower_bounds = jnp.maximum(jnp.cumsum(p, axis=0) - p[0:1], 0.0)
    cos_t, sin_t = _rope_tables(positions)
    tables = _retention_tables()
    order = _proj_column_order()
    row = lambda v: v.astype(F32)[None, :]
    pad4 = lambda v: jnp.pad(v.astype(F32), (0, LANES - GDN_HEADS))[None, :]
    for layer in range(depth):
        win = jnp.pad(w_in[layer], ((0, 0), (0, 1)))[:, order].astype(BF16)
        x = _mixer_layer(
            x, cos_t, sin_t, row(norm1_w[layer]), win, row(lower_bounds[layer]),
            gdn_conv_w[layer].astype(F32), pad4(gdn_A_log[layer]), pad4(gdn_dt_bias[layer]),
            row(jnp.tile(hg_norm_w[layer], HG_HEADS)), row(jnp.tile(gdn_norm_w[layer], GDN_HEADS)),
            w_out[layer].astype(BF16), tables)
        wup = w_up[layer].astype(BF16)
        cw, cb = ffn_conv_w[layer].astype(F32), row(ffn_conv_b[layer])
        x = _ffn_layer(
            x, row(norm2_w[layer]), wup[:, :D_FF], wup[:, D_FF:], cw[:, :D_FF], cw[:, D_FF:],
            cb[:, :D_FF], cb[:, D_FF:], w_down[layer].astype(BF16), row(final_norm_w),
            final_norm=(layer == depth - 1))
    return x
```

```python
import functools

import numpy as np
import jax
import jax.numpy as jnp
from jax import lax
from jax.experimental import pallas as pl
from jax.experimental.pallas import tpu as pltpu

F32 = jnp.float32
BF16 = jnp.bfloat16

D_MODEL = 1024
RET_HEADS = 4
RET_D = 64
HG_HEADS = 4
HG_D = 64
GDN_HEADS = 4
GDN_D = 128
GDN_CONV = 4
FFN_CONV = 3
D_FF = 2816
ROPE_BASE = 10000.0
NORM_EPS = 1e-6
EXP_CLIP = 80.0
NEG_BIG = -1e30

LANES = 128
SUBLANES = 8
TB = 256
HG_CHUNK = 16
FF_TILE = 256
D_PROJ_PAD = 4224
VMEM_LIMIT = 56 * 1024 * 1024

_RQ, _RK, _RV, _RG = 0, 256, 512, 768
_HQ, _HF, _HI, _HG = 1024, 1280, 1536, 1792
_GQKV, _GG, _GAB = 2048, 3584, 4096


def _dot(a, b):
    return jnp.dot(a.astype(BF16), b.astype(BF16), preferred_element_type=F32)


def _dot_nt(a, b):
    return lax.dot_general(a.astype(BF16), b.astype(BF16), (((1,), (1,)), ((), ())),
                           preferred_element_type=F32)


def _dot_tn(a, b):
    return lax.dot_general(a.astype(BF16), b.astype(BF16), (((0,), (0,)), ((), ())),
                           preferred_element_type=F32)


def _sigmoid(x):
    return 1.0 / (1.0 + jnp.exp(-x))


def _silu(x):
    return x * _sigmoid(x)


def _softplus(x):
    return jnp.maximum(x, 0.0) + jnp.log1p(jnp.exp(-jnp.abs(x)))


def _chunk_cumsum(x, chunk):
    row = lax.broadcasted_iota(jnp.int32, (x.shape[0], 1), 0) % chunk
    s = 1
    while s < chunk:
        x = x + jnp.where(row >= s, pltpu.roll(x, s, axis=0), 0.0)
        s *= 2
    return x


def _causal_conv(u, prev, w, width):
    acc = u * w[width - 1:width]
    row = lax.broadcasted_iota(jnp.int32, (SUBLANES, 1), 0)
    for s in range(1, width):
        rolled = pltpu.roll(u, s, axis=0)
        head = jnp.where(row < s, pltpu.roll(prev, s, axis=0), rolled[0:SUBLANES])
        shifted = jnp.concatenate([head, rolled[SUBLANES:]], axis=0)
        acc = acc + shifted * w[width - 1 - s:width - s]
    return acc


def _rope_kernel(pos_ref, invf_ref, cos_ref, sin_ref):
    ang = pos_ref[0] * invf_ref[...]
    cos_ref[0] = jnp.cos(ang)
    sin_ref[0] = jnp.sin(ang)


def _rope_tables(positions):
    B, T = positions.shape
    half = RET_D // 2
    inv_freq = ROPE_BASE ** (-jnp.arange(half, dtype=F32) / half)
    invf = jnp.tile(inv_freq, LANES // half)[None, :]
    pos = positions.astype(F32)[..., None]
    tb = 512 if T % 512 == 0 else TB
    return pl.pallas_call(
        _rope_kernel,
        grid=(B, T // tb),
        in_specs=[pl.BlockSpec((1, tb, 1), lambda b, t: (b, t, 0)),
                  pl.BlockSpec((1, LANES), lambda b, t: (0, 0))],
        out_specs=[pl.BlockSpec((1, tb, LANES), lambda b, t: (b, t, 0)),
                   pl.BlockSpec((1, tb, LANES), lambda b, t: (b, t, 0))],
        out_shape=[jax.ShapeDtypeStruct((B, T, LANES), F32)] * 2,
        name="rope_tables",
    )(pos, invf)


def _retention(proj, cos, sin, decay_ref, din_ref, dout_ref, wdec_ref, bd_ref, qmask_ref,
               vmask_ref, ones64_ref, s_scr):
    q = proj[:, _RQ:_RQ + 256]
    k = proj[:, _RK:_RK + 256]
    v = proj[:, _RV:_RV + 256]
    g = proj[:, _RG:_RG + 256]

    def rot(a):
        a1, a2 = a[:, :LANES], a[:, LANES:]
        return jnp.concatenate([a1 * cos - a2 * sin, a1 * sin + a2 * cos], axis=1)

    qr = rot(q)
    kr = rot(k) * (RET_D ** -0.5)
    kr_b = kr.astype(BF16)
    intra = jnp.zeros_like(v)
    for h in range(RET_HEADS):
        a = _dot_nt(qr * qmask_ref[h], kr_b) * decay_ref[h]
        intra = intra + _dot(a, v * vmask_ref[h])
    s = s_scr[...]
    inter = _dot(qr * din_ref[...], s)
    u = _dot_tn(kr * dout_ref[...], v)
    s_scr[...] = wdec_ref[...] * s + bd_ref[...] * u
    o = intra + inter
    ss = _dot(o * o, ones64_ref[...])
    return o * lax.rsqrt(ss * (1.0 / RET_D) + NORM_EPS) * _silu(g)


def _hgrn2(proj, lb, normw, ones64_ref, st_scr):
    hq = proj[:, _HQ:_HQ + 256]
    z = proj[:, _HF:_HF + 256]
    vi = proj[:, _HI:_HI + 256]
    gate = proj[:, _HG:_HG + 256]
    log_sig = jnp.minimum(z, 0.0) - jnp.log1p(jnp.exp(-jnp.abs(z)))
    log_f = log_sig + jnp.log1p(lb * jnp.exp(jnp.minimum(-z, EXP_CLIP)))
    q = _silu(hq)
    k = (1.0 - lb) * _sigmoid(-z)
    b = _chunk_cumsum(log_f, HG_CHUNK)
    ones64 = ones64_ref[...]
    ones64_f = ones64.astype(F32)
    C = HG_CHUNK
    row8 = lax.broadcasted_iota(jnp.int32, (SUBLANES, 1), 0)
    st = st_scr[...]
    outs = []
    for c in range(TB // C):
        r0 = c * C
        qc, kc, vc, bc = q[r0:r0 + C], k[r0:r0 + C], vi[r0:r0 + C], b[r0:r0 + C]
        blast = bc[C - 1:C, :]
        inter = _dot_nt(qc * jnp.exp(bc), st)
        u = _dot_tn(vc, kc * jnp.exp(blast - bc))
        st = jnp.exp(blast) * st + ones64_f * u
        parts = []
        for tile in range(C // SUBLANES):
            lo = tile * SUBLANES
            qt, bt = qc[lo:lo + SUBLANES], bc[lo:lo + SUBLANES]
            for j in range(lo + SUBLANES):
                diff = bt - bc[j:j + 1, :]
                if j > lo:
                    diff = jnp.where(row8 >= j - lo, diff, NEG_BIG)
                parts.append(qt * kc[j:j + 1, :] * jnp.exp(diff))
        p = jnp.concatenate(parts, axis=0).astype(BF16)
        r = jnp.dot(p, ones64, preferred_element_type=F32)
        accs = []
        off = 0
        for tile in range(C // SUBLANES):
            lo = tile * SUBLANES
            acc = inter[lo:lo + SUBLANES]
            for j in range(lo + SUBLANES):
                acc = acc + r[off:off + SUBLANES] * vc[j:j + 1, :]
                off += SUBLANES
            accs.append(acc)
        outs.extend(accs)
    st_scr[...] = st
    o = jnp.concatenate(outs, axis=0)
    ss = _dot(o * o, ones64)
    return o * lax.rsqrt(ss * (1.0 / HG_D) + NORM_EPS) * normw * _silu(gate)


def _gdn(proj, convw, alog, dtb, normw, negmask_ref, gmask_ref, carry_scr, s_scr):
    raw = proj[:, _GQKV:_GQKV + 3 * GDN_HEADS * GDN_D]
    qkv = _silu(_causal_conv(raw, carry_scr[...], convw, GDN_CONV))
    carry_scr[...] = raw[TB - SUBLANES:TB]
    W = GDN_HEADS * GDN_D
    gab = proj[:, _GAB:_GAB + LANES]
    gate = proj[:, _GG:_GG + W]
    g_all = -jnp.exp(alog) * _softplus(gab + dtb)
    beta_all = _sigmoid(gab)
    b_all = _chunk_cumsum(g_all, TB)
    b_rows = jnp.transpose(b_all)
    negmask = negmask_ref[...]
    eye = gmask_ref[0].astype(F32)

    outs = []
    for h in range(GDN_HEADS):
        lo = h * GDN_D
        qh = qkv[:, lo:lo + GDN_D]
        kh = qkv[:, W + lo:W + lo + GDN_D]
        vh = qkv[:, 2 * W + lo:2 * W + lo + GDN_D]
        qh = qh * lax.rsqrt(jnp.sum(qh * qh, axis=-1, keepdims=True) + NORM_EPS) * (GDN_D ** -0.5)
        kh = kh * lax.rsqrt(jnp.sum(kh * kh, axis=-1, keepdims=True) + NORM_EPS)
        bcol = b_all[:, h:h + 1]
        brow = b_rows[h:h + 1, :]
        beta = beta_all[:, GDN_HEADS + h:GDN_HEADS + h + 1]
        gam = jnp.exp(bcol - brow + negmask)
        kb = kh * beta
        lmat = (_dot_nt(kb, kh) * gam).astype(BF16) * gmask_ref[1]
        l8 = lmat * gmask_ref[2]
        p2 = _dot(l8, l8)
        p4 = _dot(p2, p2)
        inv = eye - l8.astype(F32)
        inv = inv + _dot(inv, p2)
        inv = inv + _dot(inv, p4)
        for m in range(3, gmask_ref.shape[0]):
            inv_b = inv.astype(BF16)
            inv = inv - _dot(_dot(inv_b, lmat * gmask_ref[m]), inv_b)
        eb = jnp.exp(bcol)
        sol = _dot(inv, jnp.concatenate([vh * beta, kb * eb], axis=1))
        u, kcd = sol[:, :GDN_D], sol[:, GDN_D:]
        attn = _dot_nt(qh, kh) * gam
        blast = bcol[TB - 1:TB, :]
        s = s_scr[h]
        v_new = u - _dot(kcd, s)
        o = _dot(qh * eb, s) + _dot(attn, v_new)
        s_scr[h] = s * jnp.exp(blast) + _dot_tn(kh * jnp.exp(blast - bcol), v_new)
        outs.append(o * lax.rsqrt(jnp.mean(o * o, axis=-1, keepdims=True) + NORM_EPS))
    return jnp.concatenate(outs, axis=1) * normw * _silu(gate)


def _mixer_kernel(x_ref, cos_ref, sin_ref, n1w_ref, win_ref, lb_ref, convw_ref, alog_ref, dtb_ref,
                  hgw_ref, gdw_ref, wout_ref, rdecay_ref, rdin_ref, rdout_ref, rwdec_ref, rbd_ref,
                  qmask_ref, vmask_ref, ones64_ref, negmask_ref, gmask_ref,
                  out_ref,
                  proj_scr, sret_scr, shg_scr, sgd_scr, carry_scr):
    @pl.when(pl.program_id(1) == 0)
    def _():
        sret_scr[...] = jnp.zeros_like(sret_scr)
        shg_scr[...] = jnp.zeros_like(shg_scr)
        sgd_scr[...] = jnp.zeros_like(sgd_scr)
        carry_scr[...] = jnp.zeros_like(carry_scr)

    x = x_ref[0]
    h = x * lax.rsqrt(jnp.mean(x * x, axis=-1, keepdims=True) + NORM_EPS) * n1w_ref[...]
    proj_scr[...] = jnp.dot(h.astype(BF16), win_ref[...], preferred_element_type=F32)
    proj = proj_scr[...]

    o_ret = _retention(proj, cos_ref[0], sin_ref[0], rdecay_ref, rdin_ref, rdout_ref, rwdec_ref,
                       rbd_ref, qmask_ref, vmask_ref, ones64_ref, sret_scr)
    o_hg = _hgrn2(proj, lb_ref[...], hgw_ref[...], ones64_ref, shg_scr)
    o_gdn = _gdn(proj, convw_ref[...], alog_ref[...], dtb_ref[...], gdw_ref[...], negmask_ref,
                 gmask_ref, carry_scr, sgd_scr)
    mixed = jnp.concatenate([o_ret, o_hg, o_gdn], axis=1).astype(BF16)
    out_ref[0] = x + jnp.dot(mixed, wout_ref[...], preferred_element_type=F32)


def _gdn_tables():
    i = np.arange(TB)[:, None]
    j = np.arange(TB)[None, :]
    negmask = np.where(i >= j, 0.0, NEG_BIG).astype(np.float32)
    masks = [i == j, i > j, (i > j) & (i // SUBLANES == j // SUBLANES)]
    s = SUBLANES
    while s < TB:
        masks.append((i // (2 * s) == j // (2 * s)) & (i // s > j // s))
        s *= 2
    return jnp.asarray(negmask), jnp.asarray(np.stack(masks).astype(np.float32), dtype=BF16)


def _retention_tables():
    hh = np.arange(RET_HEADS, dtype=np.float32)
    log_gamma = jnp.log1p(-jnp.exp2(-5.0 - jnp.asarray(hh)))
    idx = jnp.arange(TB, dtype=F32)
    rel = idx[:, None] - idx[None, :]
    causal = rel >= 0
    decay = jnp.where(causal[None], jnp.exp(jnp.where(causal[None], rel[None] * log_gamma[:, None, None], 0.0)), 0.0)
    lane = np.arange(256)
    head_qk = (lane % LANES) // (RET_D // 2)
    head_v = lane // RET_D
    lg_lane = log_gamma[head_qk]
    din = jnp.exp((idx + 1.0)[:, None] * lg_lane[None, :])
    dout = jnp.exp((TB - 1.0 - idx)[:, None] * lg_lane[None, :])
    bd = (head_qk[:, None] == head_v[None, :]).astype(np.float32)
    wdec = jnp.exp(TB * lg_lane)[:, None] * bd
    qmask = np.stack([(head_qk == h) for h in range(RET_HEADS)]).astype(np.float32)[:, None, :]
    vmask = np.stack([(head_v == h) for h in range(RET_HEADS)]).astype(np.float32)[:, None, :]
    ones64 = (head_v[:, None] == head_v[None, :]).astype(np.float32)
    return (decay, din, dout, wdec, jnp.asarray(bd), jnp.asarray(qmask), jnp.asarray(vmask),
            jnp.asarray(ones64, dtype=BF16))


def _proj_column_order():
    rq = np.arange(256)
    half, hh, j = rq // LANES, (rq % LANES) // 32, rq % 32
    rot_perm = hh * RET_D + half * 32 + j
    order = np.concatenate([
        rot_perm, 256 + rot_perm, np.arange(512, 1024),
        np.arange(1024, 2048),
        np.arange(2048, 3584),
        np.arange(3592, 4104),
        np.arange(3584, 3592),
        np.full(D_PROJ_PAD - 4104, 4104),
    ])
    assert order.shape[0] == D_PROJ_PAD
    return order


def _const_spec(shape):
    nd = len(shape)
    return pl.BlockSpec(shape, lambda b, t: (0,) * nd)


def _mixer_layer(x, cos_t, sin_t, n1w, win, lb, convw, alog, dtb, hgw, gdw, wout, tables):
    B, T, D = x.shape
    consts = [n1w, win, lb, convw, alog, dtb, hgw, gdw, wout, *tables]
    tok = lambda w: pl.BlockSpec((1, TB, w), lambda b, t: (b, t, 0))
    return pl.pallas_call(
        _mixer_kernel,
        grid=(B, T // TB),
        in_specs=[tok(D), tok(LANES), tok(LANES)] + [_const_spec(c.shape) for c in consts],
        out_specs=tok(D),
        out_shape=jax.ShapeDtypeStruct((B, T, D), F32),
        scratch_shapes=[
            pltpu.VMEM((TB, D_PROJ_PAD), F32),
            pltpu.VMEM((256, 256), F32),
            pltpu.VMEM((256, 256), F32),
            pltpu.VMEM((GDN_HEADS, GDN_D, GDN_D), F32),
            pltpu.VMEM((SUBLANES, 3 * GDN_HEADS * GDN_D), F32),
        ],
        compiler_params=pltpu.CompilerParams(
            dimension_semantics=("parallel", "arbitrary"), vmem_limit_bytes=VMEM_LIMIT),
        name="mixer_layer",
    )(x, cos_t, sin_t, *consts)


def _ffn_kernel(x_ref, n2w_ref, wg_ref, wv_ref, cwg_ref, cwv_ref, cbg_ref, cbv_ref, wd_ref, fw_ref,
                out_ref, carry_scr, act_scr, *, final_norm):
    @pl.when(pl.program_id(1) == 0)
    def _():
        carry_scr[...] = jnp.zeros_like(carry_scr)

    x = x_ref[0]
    h = (x * lax.rsqrt(jnp.mean(x * x, axis=-1, keepdims=True) + NORM_EPS) * n2w_ref[...]).astype(BF16)
    for c in range(D_FF // FF_TILE):
        cols = slice(c * FF_TILE, (c + 1) * FF_TILE)
        ys = []
        for i, (w_ref, cw_ref, cb_ref) in enumerate(((wg_ref, cwg_ref, cbg_ref), (wv_ref, cwv_ref, cbv_ref))):
            u = jnp.dot(h, w_ref[:, cols], preferred_element_type=F32)
            ys.append(_causal_conv(u, carry_scr[i, :, cols], cw_ref[:, cols], FFN_CONV) + cb_ref[:, cols])
            carry_scr[i, :, cols] = u[TB - SUBLANES:TB]
        act_scr[:, cols] = (_silu(ys[0]) * ys[1]).astype(BF16)
    o = x + jnp.dot(act_scr[...], wd_ref[...], preferred_element_type=F32)
    if final_norm:
        o = o * lax.rsqrt(jnp.mean(o * o, axis=-1, keepdims=True) + NORM_EPS) * fw_ref[...]
    out_ref[0] = o


def _ffn_layer(x, n2w, wg, wv, cwg, cwv, cbg, cbv, wd, fw, final_norm):
    B, T, D = x.shape
    consts = [n2w, wg, wv, cwg, cwv, cbg, cbv, wd, fw]
    tok = pl.BlockSpec((1, TB, D), lambda b, t: (b, t, 0))
    return pl.pallas_call(
        functools.partial(_ffn_kernel, final_norm=final_norm),
        grid=(B, T // TB),
        in_specs=[tok] + [_const_spec(c.shape) for c in consts],
        out_specs=tok,
        out_shape=jax.ShapeDtypeStruct((B, T, D), F32),
        scratch_shapes=[pltpu.VMEM((2, SUBLANES, D_FF), F32), pltpu.VMEM((TB, D_FF), BF16)],
        compiler_params=pltpu.CompilerParams(
            dimension_semantics=("parallel", "arbitrary"), vmem_limit_bytes=VMEM_LIMIT),
        name="ffn_layer",
    )(x, *consts)


def kernel(x, positions, norm1_w, w_in, hg_lb_logits, gdn_conv_w, gdn_A_log, gdn_dt_bias, hg_norm_w,
           gdn_norm_w, w_out, norm2_w, w_up, ffn_conv_w, ffn_conv_b, w_down, final_norm_w):
    depth = w_in.shape[0]
    assert x.shape[1] % TB == 0 and x.shape[2] == D_MODEL
    p = jax.nn.softmax(hg_lb_logits.astype(F32), axis=0)
    lower_bounds = jnp.maximum(jnp.cumsum(p, axis=0) - p[0:1], 0.0)
    cos_t, sin_t = _rope_tables(positions)
    tables = _retention_tables() + _gdn_tables()
    order = _proj_column_order()
    row = lambda v: v.astype(F32)[None, :]
    pad4 = lambda v: jnp.pad(v.astype(F32), (0, LANES - GDN_HEADS))[None, :]
    for layer in range(depth):
        win = jnp.pad(w_in[layer], ((0, 0), (0, 1)))[:, order].astype(BF16)
        x = _mixer_layer(
            x, cos_t, sin_t, row(norm1_w[layer]), win, row(lower_bounds[layer]),
            gdn_conv_w[layer].astype(F32), pad4(gdn_A_log[layer]), pad4(gdn_dt_bias[layer]),
            row(jnp.tile(hg_norm_w[layer], HG_HEADS)), row(jnp.tile(gdn_norm_w[layer], GDN_HEADS)),
            w_out[layer].astype(BF16), tables)
        wup = w_up[layer].astype(BF16)
        cw, cb = ffn_conv_w[layer].astype(F32), row(ffn_conv_b[layer])
        x = _ffn_layer(
            x, row(norm2_w[layer]), wup[:, :D_FF], wup[:, D_FF:], cw[:, :D_FF], cw[:, D_FF:],
            cb[:, :D_FF], cb[:, D_FF:], w_down[layer].astype(BF16), row(final_norm_w),
            final_norm=(layer == depth - 1))
    return x
```

```python
import functools

import numpy as np
import jax
import jax.numpy as jnp
from jax import lax
from jax.experimental import pallas as pl
from jax.experimental.pallas import tpu as pltpu

F32 = jnp.float32
BF16 = jnp.bfloat16

D_MODEL = 1024
RET_HEADS = 4
RET_D = 64
HG_HEADS = 4
HG_D = 64
GDN_HEADS = 4
GDN_D = 128
GDN_CONV = 4
FFN_CONV = 3
D_FF = 2816
ROPE_BASE = 10000.0
NORM_EPS = 1e-6
EXP_CLIP = 80.0
NEG_BIG = -1e30

LANES = 128
SUBLANES = 8
TB = 256
HG_CHUNK = 16
FF_TILE = 256
D_PROJ_PAD = 4224
VMEM_LIMIT = 56 * 1024 * 1024

_RQ, _RK, _RV, _RG = 0, 256, 512, 768
_HQ, _HF, _HI, _HG = 1024, 1280, 1536, 1792
_GQKV, _GG, _GAB = 2048, 3584, 4096


def _dot(a, b):
    return jnp.dot(a.astype(BF16), b.astype(BF16), preferred_element_type=F32)


def _dot_nt(a, b):
    return lax.dot_general(a.astype(BF16), b.astype(BF16), (((1,), (1,)), ((), ())),
                           preferred_element_type=F32)


def _dot_tn(a, b):
    return lax.dot_general(a.astype(BF16), b.astype(BF16), (((0,), (0,)), ((), ())),
                           preferred_element_type=F32)


def _sigmoid(x):
    return 1.0 / (1.0 + jnp.exp(-x))


def _silu(x):
    return x * _sigmoid(x)


def _softplus(x):
    return jnp.maximum(x, 0.0) + jnp.log(1.0 + jnp.exp(-jnp.abs(x)))


def _chunk_cumsum(x, chunk):
    row = lax.broadcasted_iota(jnp.int32, (x.shape[0], 1), 0) % chunk
    s = 1
    while s < chunk:
        x = x + jnp.where(row >= s, pltpu.roll(x, s, axis=0), 0.0)
        s *= 2
    return x


def _causal_conv(u, prev, w, width):
    acc = u * w[width - 1:width]
    row = lax.broadcasted_iota(jnp.int32, (SUBLANES, 1), 0)
    tiles = [prev] + [u[i:i + SUBLANES] for i in range(0, u.shape[0], SUBLANES)]
    for s in range(1, width):
        rot = [pltpu.roll(t, s, axis=0) for t in tiles]
        shifted = jnp.concatenate([jnp.where(row < s, rot[i], rot[i + 1]) for i in range(len(tiles) - 1)], axis=0)
        acc = acc + shifted * w[width - 1 - s:width - s]
    return acc


def _rope_kernel(pos_ref, invf_ref, cos_ref, sin_ref):
    ang = pos_ref[0] * invf_ref[...]
    cos_ref[0] = jnp.cos(ang)
    sin_ref[0] = jnp.sin(ang)


def _rope_tables(positions):
    B, T = positions.shape
    half = RET_D // 2
    inv_freq = ROPE_BASE ** (-jnp.arange(half, dtype=F32) / half)
    invf = jnp.tile(inv_freq, LANES // half)[None, :]
    pos = positions.astype(F32)[..., None]
    tb = 512 if T % 512 == 0 else TB
    return pl.pallas_call(
        _rope_kernel,
        grid=(B, T // tb),
        in_specs=[pl.BlockSpec((1, tb, 1), lambda b, t: (b, t, 0)),
                  pl.BlockSpec((1, LANES), lambda b, t: (0, 0))],
        out_specs=[pl.BlockSpec((1, tb, LANES), lambda b, t: (b, t, 0)),
                   pl.BlockSpec((1, tb, LANES), lambda b, t: (b, t, 0))],
        out_shape=[jax.ShapeDtypeStruct((B, T, LANES), F32)] * 2,
        name="rope_tables",
    )(pos, invf)


def _retention(project, cos, sin, decay_ref, din_ref, dout_ref, wdec_ref, bd_ref, qmask_ref,
               vmask_ref, ones64_ref, s_scr):
    proj = project(_RQ, _HQ)
    yield
    q = proj[:, 0:256]
    k = proj[:, 256:512]
    v = proj[:, 512:768]
    g = proj[:, 768:1024]

    def rot(a):
        a1, a2 = a[:, :LANES], a[:, LANES:]
        return jnp.concatenate([a1 * cos - a2 * sin, a1 * sin + a2 * cos], axis=1)

    qr = rot(q)
    kr = rot(k) * (RET_D ** -0.5)
    qr_b, kr_b, v_b = qr.astype(BF16), kr.astype(BF16), v.astype(BF16)
    yield
    intra = None
    for h in range(RET_HEADS):
        a = _dot_nt(qr_b * qmask_ref[h], kr_b) * decay_ref[h]
        part = _dot(a, v_b * vmask_ref[h])
        intra = part if intra is None else intra + part
        yield
    s = s_scr[...]
    inter = _dot(qr * din_ref[...], s)
    u = _dot_tn(kr * dout_ref[...], v)
    s_scr[...] = wdec_ref[...] * s + bd_ref[...] * u
    yield
    o = intra + inter
    ss = _dot(o * o, ones64_ref[...])
    return o * lax.rsqrt(ss * (1.0 / RET_D) + NORM_EPS) * _silu(g)


def _hgrn2(project, lb, normw, ones64_ref, st_scr):
    proj = project(_HQ, _GQKV)
    yield
    hq = proj[:, 0:256]
    z = proj[:, 256:512]
    vi = proj[:, 512:768]
    gate = proj[:, 768:1024]
    log_sig = jnp.minimum(z, 0.0) - jnp.log(1.0 + jnp.exp(-jnp.abs(z)))
    log_f = log_sig + jnp.log(1.0 + lb * jnp.exp(jnp.minimum(-z, EXP_CLIP)))
    q = _silu(hq)
    k = (1.0 - lb) * _sigmoid(-z)
    b = _chunk_cumsum(log_f, HG_CHUNK)
    ones64 = ones64_ref[...]
    ones64_f = ones64.astype(F32)
    C = HG_CHUNK
    row8 = lax.broadcasted_iota(jnp.int32, (SUBLANES, 1), 0)
    st = st_scr[...]
    outs = []
    yield
    for c in range(TB // C):
        r0 = c * C
        qc, kc, vc, bc = q[r0:r0 + C], k[r0:r0 + C], vi[r0:r0 + C], b[r0:r0 + C]
        blast = bc[C - 1:C, :]
        inter = _dot_nt(qc * jnp.exp(bc), st)
        u = _dot_tn(vc, kc * jnp.exp(blast - bc))
        st = jnp.exp(blast) * st + ones64_f * u
        parts = []
        for tile in range(C // SUBLANES):
            lo = tile * SUBLANES
            qt, bt = qc[lo:lo + SUBLANES], bc[lo:lo + SUBLANES]
            for j in range(lo + SUBLANES):
                diff = bt - bc[j:j + 1, :]
                if j > lo:
                    diff = jnp.where(row8 >= j - lo, diff, NEG_BIG)
                parts.append(qt * kc[j:j + 1, :] * jnp.exp(diff))
        p = jnp.concatenate(parts, axis=0).astype(BF16)
        r = jnp.dot(p, ones64, preferred_element_type=F32)
        accs = []
        off = 0
        for tile in range(C // SUBLANES):
            lo = tile * SUBLANES
            acc = inter[lo:lo + SUBLANES]
            for j in range(lo + SUBLANES):
                acc = acc + r[off:off + SUBLANES] * vc[j:j + 1, :]
                off += SUBLANES
            accs.append(acc)
        outs.extend(accs)
        yield
    st_scr[...] = st
    o = jnp.concatenate(outs, axis=0)
    ss = _dot(o * o, ones64)
    return o * lax.rsqrt(ss * (1.0 / HG_D) + NORM_EPS) * normw * _silu(gate)


def _gdn(project, convw, alog, dtb, normw, negmask_ref, gmask_ref, carry_scr, s_scr):
    proj = project(_GQKV, D_PROJ_PAD)
    yield
    raw = proj[:, 0:3 * GDN_HEADS * GDN_D]
    qkv = _silu(_causal_conv(raw, carry_scr[...], convw, GDN_CONV))
    carry_scr[...] = raw[TB - SUBLANES:TB]
    W = GDN_HEADS * GDN_D
    gab = proj[:, _GAB - _GQKV:_GAB - _GQKV + LANES]
    gate = proj[:, _GG - _GQKV:_GG - _GQKV + W]
    yield
    g_all = -jnp.exp(alog) * _softplus(gab + dtb)
    beta_all = _sigmoid(gab)
    b_all = _chunk_cumsum(g_all, TB)
    b_rows = jnp.transpose(b_all)
    negmask = negmask_ref[...]
    eye = gmask_ref[0].astype(F32)
    heads = range(GDN_HEADS)

    def l2n(a):
        return a * lax.rsqrt(jnp.sum(a * a, axis=-1, keepdims=True) + NORM_EPS)

    q = [l2n(qkv[:, h * GDN_D:(h + 1) * GDN_D]) * (GDN_D ** -0.5) for h in heads]
    k = [l2n(qkv[:, W + h * GDN_D:W + (h + 1) * GDN_D]) for h in heads]
    v = [qkv[:, 2 * W + h * GDN_D:2 * W + (h + 1) * GDN_D] for h in heads]
    bcol = [b_all[:, h:h + 1] for h in heads]
    beta = [beta_all[:, GDN_HEADS + h:GDN_HEADS + h + 1] for h in heads]
    gam = [jnp.exp(bcol[h] - b_rows[h:h + 1, :] + negmask) for h in heads]
    kb = [k[h] * beta[h] for h in heads]
    yield
    kq = [_dot_nt(jnp.concatenate([kb[h], q[h]], axis=0), k[h]) for h in heads]
    yield
    lmat = [(kq[h][:TB] * gam[h]).astype(BF16) * gmask_ref[1] for h in heads]
    attn = [(kq[h][TB:] * gam[h]).astype(BF16) for h in heads]
    l8 = [lmat[h] * gmask_ref[2] for h in heads]
    yield
    p2 = [_dot(l8[h], l8[h]).astype(BF16) for h in heads]
    p4 = [_dot(p2[h], p2[h]).astype(BF16) for h in heads]
    yield
    x0 = [eye - l8[h].astype(F32) for h in heads]
    x1 = [x0[h] + _dot(x0[h], p2[h]) for h in heads]
    yield
    inv = [(x1[h] + _dot(x1[h], p4[h])).astype(BF16) for h in heads]
    yield
    for m in range(3, gmask_ref.shape[0] - 1):
        t = [_dot(inv[h], lmat[h] * gmask_ref[m]).astype(BF16) for h in heads]
        yield
        inv = [inv[h] - _dot(t[h], inv[h]).astype(BF16) for h in heads]
        yield
    H2 = TB // 2
    t = [_dot(inv[h][H2:, H2:], lmat[h][H2:, :H2]).astype(BF16) for h in heads]
    yield
    low = [-_dot(t[h], inv[h][:H2, :H2]) for h in heads]
    yield
    inv = [jnp.concatenate([inv[h][:H2], jnp.concatenate([low[h].astype(BF16), inv[h][H2:, H2:]], axis=1)], axis=0)
           for h in heads]
    eb = [jnp.exp(bcol[h]) for h in heads]
    s = [s_scr[h] for h in heads]
    ks = [_dot(jnp.concatenate([kb[h] * eb[h], q[h] * eb[h]], axis=0), s[h]) for h in heads]
    yield
    v_new = [_dot(inv[h], v[h] * beta[h] - ks[h][:TB]) for h in heads]
    yield
    o = [ks[h][TB:] + _dot(attn[h], v_new[h]) for h in heads]
    yield
    for h in heads:
        blast = bcol[h][TB - 1:TB, :]
        s_scr[h] = s[h] * jnp.exp(blast) + _dot_tn(k[h] * jnp.exp(blast - bcol[h]), v_new[h])
    o = [o[h] * lax.rsqrt(jnp.mean(o[h] * o[h], axis=-1, keepdims=True) + NORM_EPS) for h in heads]
    return jnp.concatenate(o, axis=1) * normw * _silu(gate)


def _round_robin(gens):
    results = [None] * len(gens)
    live = list(range(len(gens)))
    while live:
        for i in list(live):
            try:
                next(gens[i])
            except StopIteration as stop:
                results[i] = stop.value
                live.remove(i)
    return results


def _mixer_kernel(x_ref, cos_ref, sin_ref, n1w_ref, win_ref, lb_ref, convw_ref, alog_ref, dtb_ref,
                  hgw_ref, gdw_ref, wout_ref, rdecay_ref, rdin_ref, rdout_ref, rwdec_ref, rbd_ref,
                  qmask_ref, vmask_ref, ones64_ref, negmask_ref, gmask_ref,
                  out_ref,
                  proj_scr, sret_scr, shg_scr, sgd_scr, carry_scr):
    @pl.when(pl.program_id(1) == 0)
    def _():
        sret_scr[...] = jnp.zeros_like(sret_scr)
        shg_scr[...] = jnp.zeros_like(shg_scr)
        sgd_scr[...] = jnp.zeros_like(sgd_scr)
        carry_scr[...] = jnp.zeros_like(carry_scr)

    x = x_ref[0]
    h = (x * lax.rsqrt(jnp.mean(x * x, axis=-1, keepdims=True) + NORM_EPS) * n1w_ref[...]).astype(BF16)

    def project(lo, hi):
        proj_scr[:, lo:hi] = jnp.dot(h, win_ref[:, lo:hi], preferred_element_type=F32)
        return proj_scr[:, lo:hi]

    o_gdn, o_hg, o_ret = _round_robin([
        _gdn(project, convw_ref[...], alog_ref[...], dtb_ref[...], gdw_ref[...], negmask_ref,
             gmask_ref, carry_scr, sgd_scr),
        _hgrn2(project, lb_ref[...], hgw_ref[...], ones64_ref, shg_scr),
        _retention(project, cos_ref[0], sin_ref[0], rdecay_ref, rdin_ref, rdout_ref, rwdec_ref,
                   rbd_ref, qmask_ref, vmask_ref, ones64_ref, sret_scr),
    ])
    mixed = jnp.concatenate([o_ret, o_hg, o_gdn], axis=1).astype(BF16)
    out_ref[0] = x + jnp.dot(mixed, wout_ref[...], preferred_element_type=F32)


def _gdn_tables():
    i = np.arange(TB)[:, None]
    j = np.arange(TB)[None, :]
    negmask = np.where(i >= j, 0.0, NEG_BIG).astype(np.float32)
    masks = [i == j, i > j, (i > j) & (i // SUBLANES == j // SUBLANES)]
    s = SUBLANES
    while s < TB:
        masks.append((i // (2 * s) == j // (2 * s)) & (i // s > j // s))
        s *= 2
    return jnp.asarray(negmask), jnp.asarray(np.stack(masks).astype(np.float32), dtype=BF16)


def _retention_tables():
    hh = np.arange(RET_HEADS, dtype=np.float32)
    log_gamma = jnp.log1p(-jnp.exp2(-5.0 - jnp.asarray(hh)))
    idx = jnp.arange(TB, dtype=F32)
    rel = idx[:, None] - idx[None, :]
    causal = rel >= 0
    decay = jnp.where(causal[None], jnp.exp(jnp.where(causal[None], rel[None] * log_gamma[:, None, None], 0.0)), 0.0)
    lane = np.arange(256)
    head_qk = (lane % LANES) // (RET_D // 2)
    head_v = lane // RET_D
    lg_lane = log_gamma[head_qk]
    din = jnp.exp((idx + 1.0)[:, None] * lg_lane[None, :])
    dout = jnp.exp((TB - 1.0 - idx)[:, None] * lg_lane[None, :])
    bd = (head_qk[:, None] == head_v[None, :]).astype(np.float32)
    wdec = jnp.exp(TB * lg_lane)[:, None] * bd
    qmask = np.stack([(head_qk == h) for h in range(RET_HEADS)]).astype(np.float32)[:, None, :]
    vmask = np.stack([(head_v == h) for h in range(RET_HEADS)]).astype(np.float32)[:, None, :]
    ones64 = (head_v[:, None] == head_v[None, :]).astype(np.float32)
    return (decay, din, dout, wdec, jnp.asarray(bd), jnp.asarray(qmask, dtype=BF16),
            jnp.asarray(vmask, dtype=BF16), jnp.asarray(ones64, dtype=BF16))


def _proj_column_order():
    rq = np.arange(256)
    half, hh, j = rq // LANES, (rq % LANES) // 32, rq % 32
    rot_perm = hh * RET_D + half * 32 + j
    order = np.concatenate([
        rot_perm, 256 + rot_perm, np.arange(512, 1024),
        np.arange(1024, 2048),
        np.arange(2048, 3584),
        np.arange(3592, 4104),
        np.arange(3584, 3592),
        np.full(D_PROJ_PAD - 4104, 4104),
    ])
    assert order.shape[0] == D_PROJ_PAD
    return order


def _const_spec(shape):
    nd = len(shape)
    return pl.BlockSpec(shape, lambda b, t: (0,) * nd)


def _mixer_layer(x, cos_t, sin_t, n1w, win, lb, convw, alog, dtb, hgw, gdw, wout, tables):
    B, T, D = x.shape
    consts = [n1w, win, lb, convw, alog, dtb, hgw, gdw, wout, *tables]
    tok = lambda w: pl.BlockSpec((1, TB, w), lambda b, t: (b, t, 0))
    return pl.pallas_call(
        _mixer_kernel,
        grid=(B, T // TB),
        in_specs=[tok(D), tok(LANES), tok(LANES)] + [_const_spec(c.shape) for c in consts],
        out_specs=tok(D),
        out_shape=jax.ShapeDtypeStruct((B, T, D), F32),
        scratch_shapes=[
            pltpu.VMEM((TB, D_PROJ_PAD), F32),
            pltpu.VMEM((256, 256), F32),
            pltpu.VMEM((256, 256), F32),
            pltpu.VMEM((GDN_HEADS, GDN_D, GDN_D), F32),
            pltpu.VMEM((SUBLANES, 3 * GDN_HEADS * GDN_D), F32),
        ],
        compiler_params=pltpu.CompilerParams(
            dimension_semantics=("parallel", "arbitrary"), vmem_limit_bytes=VMEM_LIMIT),
        name="mixer_layer",
    )(x, cos_t, sin_t, *consts)


def _ffn_kernel(x_ref, n2w_ref, wg_ref, wv_ref, cwg_ref, cwv_ref, cbg_ref, cbv_ref, wd_ref, fw_ref,
                out_ref, carry_scr, act_scr, *, final_norm):
    @pl.when(pl.program_id(1) == 0)
    def _():
        carry_scr[...] = jnp.zeros_like(carry_scr)

    x = x_ref[0]
    h = (x * lax.rsqrt(jnp.mean(x * x, axis=-1, keepdims=True) + NORM_EPS) * n2w_ref[...]).astype(BF16)
    for c in range(D_FF // FF_TILE):
        cols = slice(c * FF_TILE, (c + 1) * FF_TILE)
        ys = []
        for i, (w_ref, cw_ref, cb_ref) in enumerate(((wg_ref, cwg_ref, cbg_ref), (wv_ref, cwv_ref, cbv_ref))):
            u = jnp.dot(h, w_ref[:, cols], preferred_element_type=F32)
            ys.append(_causal_conv(u, carry_scr[i, :, cols], cw_ref[:, cols], FFN_CONV) + cb_ref[:, cols])
            carry_scr[i, :, cols] = u[TB - SUBLANES:TB]
        act_scr[:, cols] = (_silu(ys[0]) * ys[1]).astype(BF16)
    o = x + jnp.dot(act_scr[...], wd_ref[...], preferred_element_type=F32)
    if final_norm:
        o = o * lax.rsqrt(jnp.mean(o * o, axis=-1, keepdims=True) + NORM_EPS) * fw_ref[...]
    out_ref[0] = o


def _ffn_layer(x, n2w, wg, wv, cwg, cwv, cbg, cbv, wd, fw, final_norm):
    B, T, D = x.shape
    consts = [n2w, wg, wv, cwg, cwv, cbg, cbv, wd, fw]
    tok = pl.BlockSpec((1, TB, D), lambda b, t: (b, t, 0))
    return pl.pallas_call(
        functools.partial(_ffn_kernel, final_norm=final_norm),
        grid=(B, T // TB),
        in_specs=[tok] + [_const_spec(c.shape) for c in consts],
        out_specs=tok,
        out_shape=jax.ShapeDtypeStruct((B, T, D), F32),
        scratch_shapes=[pltpu.VMEM((2, SUBLANES, D_FF), F32), pltpu.VMEM((TB, D_FF), BF16)],
        compiler_params=pltpu.CompilerParams(
            dimension_semantics=("parallel", "arbitrary"), vmem_limit_bytes=VMEM_LIMIT),
        name="ffn_layer",
    )(x, *consts)


def kernel(x, positions, norm1_w, w_in, hg_lb_logits, gdn_conv_w, gdn_A_log, gdn_dt_bias, hg_norm_w,
           gdn_norm_w, w_out, norm2_w, w_up, ffn_conv_w, ffn_conv_b, w_down, final_norm_w):
    depth = w_in.shape[0]
    assert x.shape[1] % TB == 0 and x.shape[2] == D_MODEL
    p = jax.nn.softmax(hg_lb_logits.astype(F32), axis=0)
    lower_bounds = jnp.maximum(jnp.cumsum(p, axis=0) - p[0:1], 0.0)
    cos_t, sin_t = _rope_tables(positions)
    tables = _retention_tables() + _gdn_tables()
    order = _proj_column_order()
    row = lambda v: v.astype(F32)[None, :]
    pad4 = lambda v: jnp.pad(v.astype(F32), (0, LANES - GDN_HEADS))[None, :]
    for layer in range(depth):
        win = jnp.pad(w_in[layer], ((0, 0), (0, 1)))[:, order].astype(BF16)
        x = _mixer_layer(
            x, cos_t, sin_t, row(norm1_w[layer]), win, row(lower_bounds[layer]),
            gdn_conv_w[layer].astype(F32), pad4(gdn_A_log[layer]), pad4(gdn_dt_bias[layer]),
            row(jnp.tile(hg_norm_w[layer], HG_HEADS)), row(jnp.tile(gdn_norm_w[layer], GDN_HEADS)),
            w_out[layer].astype(BF16), tables)
        wup = w_up[layer].astype(BF16)
        cw, cb = ffn_conv_w[layer].astype(F32), row(ffn_conv_b[layer])
        x = _ffn_layer(
            x, row(norm2_w[layer]), wup[:, :D_FF], wup[:, D_FF:], cw[:, :D_FF], cw[:, D_FF:],
            cb[:, :D_FF], cb[:, D_FF:], w_down[layer].astype(BF16), row(final_norm_w),
            final_norm=(layer == depth - 1))
    return x
```

```python
import functools

import numpy as np
import jax
import jax.numpy as jnp
from jax import lax
from jax.experimental import pallas as pl
from jax.experimental.pallas import tpu as pltpu

F32 = jnp.float32
BF16 = jnp.bfloat16

D_MODEL = 1024
RET_HEADS = 4
RET_D = 64
HG_HEADS = 4
HG_D = 64
GDN_HEADS = 4
GDN_D = 128
GDN_CONV = 4
FFN_CONV = 3
D_FF = 2816
ROPE_BASE = 10000.0
NORM_EPS = 1e-6
EXP_CLIP = 80.0
NEG_BIG = -1e30

LANES = 128
SUBLANES = 8
TB = 256
HG_CHUNK = 16
FF_TILE = 256
FFN_TB = 512
D_PROJ_PAD = 4224
VMEM_LIMIT = 56 * 1024 * 1024

_RQ, _RK, _RV, _RG = 0, 256, 512, 768
_HQ, _HF, _HI, _HG = 1024, 1280, 1536, 1792
_GQKV, _GG, _GAB = 2048, 3584, 4096


def _dot(a, b):
    return jnp.dot(a.astype(BF16), b.astype(BF16), preferred_element_type=F32)


def _dot_nt(a, b):
    return lax.dot_general(a.astype(BF16), b.astype(BF16), (((1,), (1,)), ((), ())),
                           preferred_element_type=F32)


def _dot_tn(a, b):
    return lax.dot_general(a.astype(BF16), b.astype(BF16), (((0,), (0,)), ((), ())),
                           preferred_element_type=F32)


def _sigmoid(x):
    return 1.0 / (1.0 + jnp.exp(-x))


def _silu(x):
    return x * _sigmoid(x)


def _softplus(x):
    return jnp.maximum(x, 0.0) + jnp.log(1.0 + jnp.exp(-jnp.abs(x)))


def _chunk_cumsum(x, chunk):
    row = lax.broadcasted_iota(jnp.int32, (x.shape[0], 1), 0) % chunk
    s = 1
    while s < chunk:
        x = x + jnp.where(row >= s, pltpu.roll(x, s, axis=0), 0.0)
        s *= 2
    return x


def _causal_conv(u, prev, w, width):
    acc = u * w[width - 1:width]
    row = lax.broadcasted_iota(jnp.int32, (SUBLANES, 1), 0)
    tiles = [prev] + [u[i:i + SUBLANES] for i in range(0, u.shape[0], SUBLANES)]
    for s in range(1, width):
        rot = [pltpu.roll(t, s, axis=0) for t in tiles]
        shifted = jnp.concatenate([jnp.where(row < s, rot[i], rot[i + 1]) for i in range(len(tiles) - 1)], axis=0)
        acc = acc + shifted * w[width - 1 - s:width - s]
    return acc


def _rope_kernel(pos_ref, invf_ref, cos_ref, sin_ref):
    ang = pos_ref[0] * invf_ref[...]
    cos_ref[0] = jnp.cos(ang)
    sin_ref[0] = jnp.sin(ang)


def _rope_tables(positions):
    B, T = positions.shape
    half = RET_D // 2
    rep = LANES // half
    inv_freq = ROPE_BASE ** (-jnp.arange(half, dtype=F32) / half)
    invf = jnp.tile(inv_freq, rep)[None, :]
    pos = jnp.repeat(positions.astype(F32).reshape(B, T // rep, rep), half, axis=-1)
    rows = T // rep
    tb = 512 if rows % 512 == 0 else rows
    spec = pl.BlockSpec((1, tb, LANES), lambda b, t: (b, t, 0))
    cos_t, sin_t = pl.pallas_call(
        _rope_kernel,
        grid=(B, rows // tb),
        in_specs=[spec, pl.BlockSpec((1, LANES), lambda b, t: (0, 0))],
        out_specs=[spec, spec],
        out_shape=[jax.ShapeDtypeStruct((B, rows, LANES), F32)] * 2,
        name="rope_tables",
    )(pos, invf)
    return cos_t.reshape(B, T, half), sin_t.reshape(B, T, half)


def _retention(project, cos, sin, decay_ref, din_ref, dout_ref, wdec_ref, bd_ref, qmask_ref,
               vmask_ref, ones64_ref, s_scr):
    proj = project(_RQ, _HQ)
    yield
    q = proj[:, 0:256]
    k = proj[:, 256:512]
    v = proj[:, 512:768]
    g = proj[:, 768:1024]

    cos = jnp.concatenate([cos] * RET_HEADS, axis=1)
    sin = jnp.concatenate([sin] * RET_HEADS, axis=1)

    def rot(a):
        a1, a2 = a[:, :LANES], a[:, LANES:]
        return jnp.concatenate([a1 * cos - a2 * sin, a1 * sin + a2 * cos], axis=1)

    qr = rot(q)
    kr = rot(k) * (RET_D ** -0.5)
    qr_b, kr_b, v_b = qr.astype(BF16), kr.astype(BF16), v.astype(BF16)
    yield
    intra = None
    for h in range(RET_HEADS):
        a = _dot_nt(qr_b * qmask_ref[h], kr_b) * decay_ref[h]
        part = _dot(a, v_b * vmask_ref[h])
        intra = part if intra is None else intra + part
        yield
    s = s_scr[...]
    inter = _dot(qr * din_ref[...], s)
    u = _dot_tn(kr * dout_ref[...], v)
    s_scr[...] = wdec_ref[...] * s + bd_ref[...] * u
    yield
    o = intra + inter
    ss = _dot(o * o, ones64_ref[...])
    return o * lax.rsqrt(ss * (1.0 / RET_D) + NORM_EPS) * _silu(g)


def _hgrn2(project, lb, normw, ones64_ref, st_scr):
    proj = project(_HQ, _GQKV)
    yield
    hq = proj[:, 0:256]
    z = proj[:, 256:512]
    vi = proj[:, 512:768]
    gate = proj[:, 768:1024]
    log_sig = jnp.minimum(z, 0.0) - jnp.log(1.0 + jnp.exp(-jnp.abs(z)))
    log_f = log_sig + jnp.log(1.0 + lb * jnp.exp(jnp.minimum(-z, EXP_CLIP)))
    q = _silu(hq)
    k = (1.0 - lb) * _sigmoid(-z)
    b = _chunk_cumsum(log_f, HG_CHUNK)
    ones64 = ones64_ref[...]
    ones64_f = ones64.astype(F32)
    C = HG_CHUNK
    row8 = lax.broadcasted_iota(jnp.int32, (SUBLANES, 1), 0)
    st = st_scr[...]
    outs = []
    yield
    for c in range(TB // C):
        r0 = c * C
        qc, kc, vc, bc = q[r0:r0 + C], k[r0:r0 + C], vi[r0:r0 + C], b[r0:r0 + C]
        blast = bc[C - 1:C, :]
        inter = _dot_nt(qc * jnp.exp(bc), st)
        u = _dot_tn(vc, kc * jnp.exp(blast - bc))
        st = jnp.exp(blast) * st + ones64_f * u
        parts = []
        for tile in range(C // SUBLANES):
            lo = tile * SUBLANES
            qt, bt = qc[lo:lo + SUBLANES], bc[lo:lo + SUBLANES]
            for j in range(lo + SUBLANES):
                diff = bt - bc[j:j + 1, :]
                if j > lo:
                    diff = jnp.where(row8 >= j - lo, diff, NEG_BIG)
                parts.append(qt * kc[j:j + 1, :] * jnp.exp(diff))
        p = jnp.concatenate(parts, axis=0).astype(BF16)
        r = jnp.dot(p, ones64, preferred_element_type=F32)
        accs = []
        off = 0
        for tile in range(C // SUBLANES):
            lo = tile * SUBLANES
            acc = inter[lo:lo + SUBLANES]
            for j in range(lo + SUBLANES):
                acc = acc + r[off:off + SUBLANES] * vc[j:j + 1, :]
                off += SUBLANES
            accs.append(acc)
        outs.extend(accs)
        yield
    st_scr[...] = st
    o = jnp.concatenate(outs, axis=0)
    ss = _dot(o * o, ones64)
    return o * lax.rsqrt(ss * (1.0 / HG_D) + NORM_EPS) * normw * _silu(gate)


def _gdn(project, convw, alog, dtb, normw, negmask_ref, gmask_ref, carry_scr, s_scr):
    proj = project(_GQKV, D_PROJ_PAD)
    yield
    raw = proj[:, 0:3 * GDN_HEADS * GDN_D]
    qkv = _silu(_causal_conv(raw, carry_scr[...], convw, GDN_CONV))
    carry_scr[...] = raw[TB - SUBLANES:TB]
    W = GDN_HEADS * GDN_D
    gab = proj[:, _GAB - _GQKV:_GAB - _GQKV + LANES]
    gate = proj[:, _GG - _GQKV:_GG - _GQKV + W]
    yield
    g_all = -jnp.exp(alog) * _softplus(gab + dtb)
    beta_all = _sigmoid(gab)
    b_all = _chunk_cumsum(g_all, TB)
    b_rows = jnp.transpose(b_all)
    negmask = negmask_ref[...]
    eye = gmask_ref[0].astype(F32)
    heads = range(GDN_HEADS)

    def l2n(a):
        return a * lax.rsqrt(jnp.sum(a * a, axis=-1, keepdims=True) + NORM_EPS)

    q = [l2n(qkv[:, h * GDN_D:(h + 1) * GDN_D]) * (GDN_D ** -0.5) for h in heads]
    k = [l2n(qkv[:, W + h * GDN_D:W + (h + 1) * GDN_D]) for h in heads]
    v = [qkv[:, 2 * W + h * GDN_D:2 * W + (h + 1) * GDN_D] for h in heads]
    bcol = [b_all[:, h:h + 1] for h in heads]
    beta = [beta_all[:, GDN_HEADS + h:GDN_HEADS + h + 1] for h in heads]
    gam = [jnp.exp(bcol[h] - b_rows[h:h + 1, :] + negmask) for h in heads]
    kb = [k[h] * beta[h] for h in heads]
    yield
    kq = [_dot_nt(jnp.concatenate([kb[h], q[h]], axis=0), k[h]) for h in heads]
    yield
    lmat = [(kq[h][:TB] * gam[h]).astype(BF16) * gmask_ref[1] for h in heads]
    attn = [(kq[h][TB:] * gam[h]).astype(BF16) for h in heads]
    l8 = [lmat[h] * gmask_ref[2] for h in heads]
    yield
    p2 = [_dot(l8[h], l8[h]).astype(BF16) for h in heads]
    p4 = [_dot(p2[h], p2[h]).astype(BF16) for h in heads]
    yield
    x0 = [eye - l8[h].astype(F32) for h in heads]
    x1 = [x0[h] + _dot(x0[h], p2[h]) for h in heads]
    yield
    inv = [(x1[h] + _dot(x1[h], p4[h])).astype(BF16) for h in heads]
    yield
    t = [_dot(inv[h], lmat[h] * gmask_ref[3]).astype(BF16) for h in heads]
    yield
    inv = [inv[h] - _dot(t[h], inv[h]).astype(BF16) for h in heads]
    yield
    sb = 2 * SUBLANES
    for m in range(4, gmask_ref.shape[0]):
        odd = [slice(r, r + sb) for r in range(sb, TB, 2 * sb)]
        take = lambda a: jnp.concatenate([a[r] for r in odd], axis=0) if len(odd) > 1 else a[odd[0]]
        t = [_dot(take(inv[h]), lmat[h] * gmask_ref[m]).astype(BF16) for h in heads]
        yield
        new_rows = [take(inv[h]) - _dot(t[h], inv[h]).astype(BF16) for h in heads]
        yield
        rebuilt = []
        for h in heads:
            parts = []
            for n in range(TB // (2 * sb)):
                parts.append(inv[h][2 * n * sb:(2 * n + 1) * sb])
                parts.append(new_rows[h][n * sb:(n + 1) * sb])
            rebuilt.append(jnp.concatenate(parts, axis=0))
        inv = rebuilt
        sb *= 2
    eb = [jnp.exp(bcol[h]) for h in heads]
    s = [s_scr[h] for h in heads]
    zero = jnp.zeros((GDN_D, GDN_D), BF16)
    ks = [None] * GDN_HEADS
    for h0 in range(0, GDN_HEADS, 2):
        h1 = h0 + 1
        lhs = jnp.concatenate([
            jnp.concatenate([kb[h0] * eb[h0], kb[h1] * eb[h1]], axis=1),
            jnp.concatenate([q[h0] * eb[h0], q[h1] * eb[h1]], axis=1)], axis=0)
        s0, s1 = s[h0].astype(BF16), s[h1].astype(BF16)
        rhs = jnp.concatenate([jnp.concatenate([s0, zero], axis=1), jnp.concatenate([zero, s1], axis=1)], axis=0)
        pair = _dot(lhs, rhs)
        ks[h0], ks[h1] = pair[:, :GDN_D], pair[:, GDN_D:]
    yield
    v_new = [_dot(inv[h], v[h] * beta[h] - ks[h][:TB]) for h in heads]
    yield
    o = [ks[h][TB:] + _dot(attn[h], v_new[h]) for h in heads]
    yield
    for h in heads:
        blast = bcol[h][TB - 1:TB, :]
        s_scr[h] = s[h] * jnp.exp(blast) + _dot_tn(k[h] * jnp.exp(blast - bcol[h]), v_new[h])
    o = [o[h] * lax.rsqrt(jnp.mean(o[h] * o[h], axis=-1, keepdims=True) + NORM_EPS) for h in heads]
    return jnp.concatenate(o, axis=1) * normw * _silu(gate)


def _round_robin(gens):
    results = [None] * len(gens)
    live = list(range(len(gens)))
    while live:
        for i in list(live):
            try:
                next(gens[i])
            except StopIteration as stop:
                results[i] = stop.value
                live.remove(i)
    return results


def _mixer_kernel(x_ref, cos_ref, sin_ref, n1w_ref, win_ref, lb_ref, convw_ref, alog_ref, dtb_ref,
                  hgw_ref, gdw_ref, wout_ref, rdecay_ref, rdin_ref, rdout_ref, rwdec_ref, rbd_ref,
                  qmask_ref, vmask_ref, ones64_ref, negmask_ref, gmask_ref,
                  out_ref,
                  proj_scr, sret_scr, shg_scr, sgd_scr, carry_scr):
    @pl.when(pl.program_id(1) == 0)
    def _():
        sret_scr[...] = jnp.zeros_like(sret_scr)
        shg_scr[...] = jnp.zeros_like(shg_scr)
        sgd_scr[...] = jnp.zeros_like(sgd_scr)
        carry_scr[...] = jnp.zeros_like(carry_scr)

    x = x_ref[0]
    h = (x * lax.rsqrt(jnp.mean(x * x, axis=-1, keepdims=True) + NORM_EPS) * n1w_ref[...]).astype(BF16)

    def project(lo, hi):
        proj_scr[:, lo:hi] = jnp.dot(h, win_ref[:, lo:hi], preferred_element_type=F32)
        return proj_scr[:, lo:hi]

    o_gdn, o_hg, o_ret = _round_robin([
        _gdn(project, convw_ref[...], alog_ref[...], dtb_ref[...], gdw_ref[...], negmask_ref,
             gmask_ref, carry_scr, sgd_scr),
        _hgrn2(project, lb_ref[...], hgw_ref[...], ones64_ref, shg_scr),
        _retention(project, cos_ref[0], sin_ref[0], rdecay_ref, rdin_ref, rdout_ref, rwdec_ref,
                   rbd_ref, qmask_ref, vmask_ref, ones64_ref, sret_scr),
    ])
    mixed = jnp.concatenate([o_ret, o_hg, o_gdn], axis=1).astype(BF16)
    out_ref[0] = x + jnp.dot(mixed, wout_ref[...], preferred_element_type=F32)


def _gdn_tables():
    i = np.arange(TB)[:, None]
    j = np.arange(TB)[None, :]
    negmask = np.where(i >= j, 0.0, NEG_BIG).astype(np.float32)
    masks = [i == j, i > j, (i > j) & (i // SUBLANES == j // SUBLANES)]
    s = SUBLANES
    while s < TB:
        masks.append((i // (2 * s) == j // (2 * s)) & (i // s > j // s))
        s *= 2
    return jnp.asarray(negmask), jnp.asarray(np.stack(masks).astype(np.float32), dtype=BF16)


def _retention_tables():
    hh = np.arange(RET_HEADS, dtype=np.float32)
    log_gamma = jnp.log1p(-jnp.exp2(-5.0 - jnp.asarray(hh)))
    idx = jnp.arange(TB, dtype=F32)
    rel = idx[:, None] - idx[None, :]
    causal = rel >= 0
    decay = jnp.where(causal[None], jnp.exp(jnp.where(causal[None], rel[None] * log_gamma[:, None, None], 0.0)), 0.0)
    lane = np.arange(256)
    head_qk = (lane % LANES) // (RET_D // 2)
    head_v = lane // RET_D
    lg_lane = log_gamma[head_qk]
    din = jnp.exp((idx + 1.0)[:, None] * lg_lane[None, :])
    dout = jnp.exp((TB - 1.0 - idx)[:, None] * lg_lane[None, :])
    bd = (head_qk[:, None] == head_v[None, :]).astype(np.float32)
    wdec = jnp.exp(TB * lg_lane)[:, None] * bd
    qmask = np.stack([(head_qk == h) for h in range(RET_HEADS)]).astype(np.float32)[:, None, :]
    vmask = np.stack([(head_v == h) for h in range(RET_HEADS)]).astype(np.float32)[:, None, :]
    ones64 = (head_v[:, None] == head_v[None, :]).astype(np.float32)
    return (decay, din, dout, wdec, jnp.asarray(bd), jnp.asarray(qmask, dtype=BF16),
            jnp.asarray(vmask, dtype=BF16), jnp.asarray(ones64, dtype=BF16))


def _regroup_projection(w_in):
    w = w_in.astype(BF16)
    half = RET_D // 2
    pieces = []
    for base in (0, RET_HEADS * RET_D):
        for part in range(2):
            pieces += [w[..., base + h * RET_D + part * half:base + h * RET_D + (part + 1) * half]
                       for h in range(RET_HEADS)]
    pieces += [w[..., 512:3584], w[..., 3592:4104], w[..., 3584:3592]]
    pieces.append(jnp.zeros(w.shape[:-1] + (D_PROJ_PAD - w.shape[-1],), BF16))
    return jnp.concatenate(pieces, axis=-1)


def _const_spec(shape):
    nd = len(shape)
    return pl.BlockSpec(shape, lambda b, t: (0,) * nd)


def _layer_spec(stacked, layer):
    nd = stacked.ndim
    return pl.BlockSpec((None,) + stacked.shape[1:], lambda b, t: (layer,) + (0,) * (nd - 1))


def _mixer_layer(x, cos_t, sin_t, layer, stacked, tables):
    B, T, D = x.shape
    tok = lambda w: pl.BlockSpec((1, TB, w), lambda b, t: (b, t, 0))
    return pl.pallas_call(
        _mixer_kernel,
        grid=(B, T // TB),
        in_specs=([tok(D), tok(RET_D // 2), tok(RET_D // 2)] + [_layer_spec(a, layer) for a in stacked]
                  + [_const_spec(c.shape) for c in tables]),
        out_specs=tok(D),
        out_shape=jax.ShapeDtypeStruct((B, T, D), F32),
        scratch_shapes=[
            pltpu.VMEM((TB, D_PROJ_PAD), F32),
            pltpu.VMEM((256, 256), F32),
            pltpu.VMEM((256, 256), F32),
            pltpu.VMEM((GDN_HEADS, GDN_D, GDN_D), F32),
            pltpu.VMEM((SUBLANES, 3 * GDN_HEADS * GDN_D), F32),
        ],
        compiler_params=pltpu.CompilerParams(
            dimension_semantics=("parallel", "arbitrary"), vmem_limit_bytes=VMEM_LIMIT),
        name="mixer_layer",
    )(x, cos_t, sin_t, *stacked, *tables)


def _ffn_kernel(x_ref, n2w_ref, wup_ref, cw_ref, cb_ref, wd_ref, fw_ref, out_ref, carry_scr, act_scr, *,
                final_norm):
    @pl.when(pl.program_id(1) == 0)
    def _():
        carry_scr[...] = jnp.zeros_like(carry_scr)

    x = x_ref[0]
    h = (x * lax.rsqrt(jnp.mean(x * x, axis=-1, keepdims=True) + NORM_EPS) * n2w_ref[...]).astype(BF16)
    for c in range(D_FF // FF_TILE):
        cols = slice(c * FF_TILE, (c + 1) * FF_TILE)
        ys = []
        for off in (0, D_FF):
            wide = slice(off + c * FF_TILE, off + (c + 1) * FF_TILE)
            u = jnp.dot(h, wup_ref[:, wide], preferred_element_type=F32)
            ys.append(_causal_conv(u, carry_scr[:, wide], cw_ref[:, wide], FFN_CONV) + cb_ref[:, wide])
            carry_scr[:, wide] = u[FFN_TB - SUBLANES:FFN_TB]
        act_scr[:, cols] = (_silu(ys[0]) * ys[1]).astype(BF16)
    o = x + jnp.dot(act_scr[...], wd_ref[...], preferred_element_type=F32)
    if final_norm:
        o = o * lax.rsqrt(jnp.mean(o * o, axis=-1, keepdims=True) + NORM_EPS) * fw_ref[...]
    out_ref[0] = o


def _ffn_layer(x, layer, n2w, wup, cw, cb, wd, fw, final_norm):
    B, T, D = x.shape
    stacked = [n2w, wup, cw, cb, wd]
    tok = pl.BlockSpec((1, FFN_TB, D), lambda b, t: (b, t, 0))
    return pl.pallas_call(
        functools.partial(_ffn_kernel, final_norm=final_norm),
        grid=(B, T // FFN_TB),
        in_specs=[tok] + [_layer_spec(a, layer) for a in stacked] + [_const_spec(fw.shape)],
        out_specs=tok,
        out_shape=jax.ShapeDtypeStruct((B, T, D), F32),
        scratch_shapes=[pltpu.VMEM((SUBLANES, 2 * D_FF), F32), pltpu.VMEM((FFN_TB, D_FF), BF16)],
        compiler_params=pltpu.CompilerParams(
            dimension_semantics=("parallel", "arbitrary"), vmem_limit_bytes=VMEM_LIMIT),
        name="ffn_layer",
    )(x, *stacked, fw)


def kernel(x, positions, norm1_w, w_in, hg_lb_logits, gdn_conv_w, gdn_A_log, gdn_dt_bias, hg_norm_w,
           gdn_norm_w, w_out, norm2_w, w_up, ffn_conv_w, ffn_conv_b, w_down, final_norm_w):
    depth = w_in.shape[0]
    assert x.shape[1] % TB == 0 and x.shape[1] % FFN_TB == 0 and x.shape[2] == D_MODEL
    p = jax.nn.softmax(hg_lb_logits.astype(F32), axis=0)
    lower_bounds = jnp.maximum(jnp.cumsum(p, axis=0) - p[0:1], 0.0)
    cos_t, sin_t = _rope_tables(positions)
    tables = _retention_tables() + _gdn_tables()
    rows = lambda v: v.astype(F32)[:, None, :]
    pad4 = lambda v: jnp.pad(v.astype(F32), ((0, 0), (0, LANES - GDN_HEADS)))[:, None, :]
    mixer_params = [
        rows(norm1_w), _regroup_projection(w_in), rows(lower_bounds), gdn_conv_w.astype(F32),
        pad4(gdn_A_log), pad4(gdn_dt_bias), rows(jnp.tile(hg_norm_w, (1, HG_HEADS))),
        rows(jnp.tile(gdn_norm_w, (1, GDN_HEADS))), w_out.astype(BF16)]
    ffn_params = [rows(norm2_w), w_up.astype(BF16), ffn_conv_w.astype(F32), rows(ffn_conv_b),
                  w_down.astype(BF16)]
    fw = final_norm_w.astype(F32)[None, :]
    for layer in range(depth):
        x = _mixer_layer(x, cos_t, sin_t, layer, mixer_params, tables)
        x = _ffn_layer(x, layer, *ffn_params, fw, final_norm=(layer == depth - 1))
    return x
```

```python
import functools

import numpy as np
import jax
import jax.numpy as jnp
from jax import lax
from jax.experimental import pallas as pl
from jax.experimental.pallas import tpu as pltpu

F32 = jnp.float32
BF16 = jnp.bfloat16

D_MODEL = 1024
RET_HEADS = 4
RET_D = 64
HG_HEADS = 4
HG_D = 64
GDN_HEADS = 4
GDN_D = 128
GDN_CONV = 4
FFN_CONV = 3
D_FF = 2816
ROPE_BASE = 10000.0
NORM_EPS = 1e-6
EXP_CLIP = 80.0
NEG_BIG = -1e30

LANES = 128
SUBLANES = 8
TB = 256
HG_CHUNK = 16
FF_TILE = 256
FFN_TB = 512
D_PROJ_PAD = 4224
VMEM_LIMIT = 56 * 1024 * 1024

_RQ, _RK, _RV, _RG = 0, 256, 512, 768
_HQ, _HF, _HI, _HG = 1024, 1280, 1536, 1792
_GQKV, _GG, _GAB = 2048, 3584, 4096


def _dot(a, b):
    return jnp.dot(a.astype(BF16), b.astype(BF16), preferred_element_type=F32)


def _dot_nt(a, b):
    return lax.dot_general(a.astype(BF16), b.astype(BF16), (((1,), (1,)), ((), ())),
                           preferred_element_type=F32)


def _dot_tn(a, b):
    return lax.dot_general(a.astype(BF16), b.astype(BF16), (((0,), (0,)), ((), ())),
                           preferred_element_type=F32)


def _sigmoid(x):
    return 1.0 / (1.0 + jnp.exp(-x))


def _silu(x):
    return x * _sigmoid(x)


def _softplus(x):
    return jnp.maximum(x, 0.0) + jnp.log(1.0 + jnp.exp(-jnp.abs(x)))


def _chunk_cumsum(x, chunk):
    row = lax.broadcasted_iota(jnp.int32, (x.shape[0], 1), 0) % chunk
    s = 1
    while s < chunk:
        x = x + jnp.where(row >= s, pltpu.roll(x, s, axis=0), 0.0)
        s *= 2
    return x


def _causal_conv(u, prev, w, width):
    acc = u * w[width - 1:width]
    row = lax.broadcasted_iota(jnp.int32, (SUBLANES, 1), 0)
    tiles = [prev] + [u[i:i + SUBLANES] for i in range(0, u.shape[0], SUBLANES)]
    for s in range(1, width):
        rot = [pltpu.roll(t, s, axis=0) for t in tiles]
        shifted = jnp.concatenate([jnp.where(row < s, rot[i], rot[i + 1]) for i in range(len(tiles) - 1)], axis=0)
        acc = acc + shifted * w[width - 1 - s:width - s]
    return acc


def _rope_kernel(pos_ref, invf_ref, cos_ref, sin_ref):
    ang = pos_ref[0] * invf_ref[...]
    cos_ref[0] = jnp.cos(ang)
    sin_ref[0] = jnp.sin(ang)


def _rope_tables(positions):
    B, T = positions.shape
    half = RET_D // 2
    rep = LANES // half
    inv_freq = ROPE_BASE ** (-jnp.arange(half, dtype=F32) / half)
    invf = jnp.tile(inv_freq, rep)[None, :]
    pos = jnp.repeat(positions.astype(F32).reshape(B, T // rep, rep), half, axis=-1)
    rows = T // rep
    tb = 512 if rows % 512 == 0 else rows
    spec = pl.BlockSpec((1, tb, LANES), lambda b, t: (b, t, 0))
    cos_t, sin_t = pl.pallas_call(
        _rope_kernel,
        grid=(B, rows // tb),
        in_specs=[spec, pl.BlockSpec((1, LANES), lambda b, t: (0, 0))],
        out_specs=[spec, spec],
        out_shape=[jax.ShapeDtypeStruct((B, rows, LANES), F32)] * 2,
        name="rope_tables",
    )(pos, invf)
    return cos_t.reshape(B, T, half), sin_t.reshape(B, T, half)


def _retention(project, cos, sin, decay_ref, din_ref, dout_ref, wdec_ref, bd_ref, qmask_ref,
               vmask_ref, ones64_ref, s_scr):
    proj = project(_RQ, _HQ)
    yield
    q = proj[:, 0:256]
    k = proj[:, 256:512]
    v = proj[:, 512:768]
    g = proj[:, 768:1024]

    cos = jnp.concatenate([cos] * RET_HEADS, axis=1)
    sin = jnp.concatenate([sin] * RET_HEADS, axis=1)

    def rot(a):
        a1, a2 = a[:, :LANES], a[:, LANES:]
        return jnp.concatenate([a1 * cos - a2 * sin, a1 * sin + a2 * cos], axis=1)

    qr = rot(q)
    kr = rot(k) * (RET_D ** -0.5)
    qr_b, kr_b, v_b = qr.astype(BF16), kr.astype(BF16), v.astype(BF16)
    yield
    intra = None
    for h in range(RET_HEADS):
        a = _dot_nt(qr_b * qmask_ref[h], kr_b) * decay_ref[h]
        part = _dot(a, v_b * vmask_ref[h])
        intra = part if intra is None else intra + part
        yield
    s = s_scr[...]
    inter = _dot(qr * din_ref[...], s)
    u = _dot_tn(kr * dout_ref[...], v)
    s_scr[...] = wdec_ref[...] * s + bd_ref[...] * u
    yield
    o = intra + inter
    ss = _dot(o * o, ones64_ref[...])
    return o * lax.rsqrt(ss * (1.0 / RET_D) + NORM_EPS) * _silu(g)


def _hgrn2(project, lb, normw, ones64_ref, st_scr):
    proj = project(_HQ, _GQKV)
    yield
    hq = proj[:, 0:256]
    z = proj[:, 256:512]
    vi = proj[:, 512:768]
    gate = proj[:, 768:1024]
    log_sig = jnp.minimum(z, 0.0) - jnp.log(1.0 + jnp.exp(-jnp.abs(z)))
    log_f = log_sig + jnp.log(1.0 + lb * jnp.exp(jnp.minimum(-z, EXP_CLIP)))
    q = _silu(hq)
    k = (1.0 - lb) * _sigmoid(-z)
    b = _chunk_cumsum(log_f, HG_CHUNK)
    ones64 = ones64_ref[...]
    ones64_f = ones64.astype(F32)
    C = HG_CHUNK
    row8 = lax.broadcasted_iota(jnp.int32, (SUBLANES, 1), 0)
    st = st_scr[...]
    outs = []
    yield
    for c in range(TB // C):
        r0 = c * C
        qc, kc, vc, bc = q[r0:r0 + C], k[r0:r0 + C], vi[r0:r0 + C], b[r0:r0 + C]
        blast = bc[C - 1:C, :]
        inter = _dot_nt(qc * jnp.exp(bc), st)
        u = _dot_tn(vc, kc * jnp.exp(blast - bc))
        st = jnp.exp(blast) * st + ones64_f * u
        parts = []
        for tile in range(C // SUBLANES):
            lo = tile * SUBLANES
            qt, bt = qc[lo:lo + SUBLANES], bc[lo:lo + SUBLANES]
            for j in range(lo + SUBLANES):
                diff = bt - bc[j:j + 1, :]
                if j > lo:
                    diff = jnp.where(row8 >= j - lo, diff, NEG_BIG)
                parts.append(qt * kc[j:j + 1, :] * jnp.exp(diff))
        p = jnp.concatenate(parts, axis=0).astype(BF16)
        r = jnp.dot(p, ones64, preferred_element_type=F32)
        accs = []
        off = 0
        for tile in range(C // SUBLANES):
            lo = tile * SUBLANES
            acc = inter[lo:lo + SUBLANES]
            for j in range(lo + SUBLANES):
                acc = acc + r[off:off + SUBLANES] * vc[j:j + 1, :]
                off += SUBLANES
            accs.append(acc)
        outs.extend(accs)
        yield
    st_scr[...] = st
    o = jnp.concatenate(outs, axis=0)
    ss = _dot(o * o, ones64)
    return o * lax.rsqrt(ss * (1.0 / HG_D) + NORM_EPS) * normw * _silu(gate)


def _gdn(project, convw, alog, dtb, normw, negmask_ref, gmask_ref, carry_scr, s_scr):
    proj = project(_GQKV, D_PROJ_PAD)
    yield
    raw = proj[:, 0:3 * GDN_HEADS * GDN_D]
    qkv = _silu(_causal_conv(raw, carry_scr[...], convw, GDN_CONV))
    carry_scr[...] = raw[TB - SUBLANES:TB]
    W = GDN_HEADS * GDN_D
    gab = proj[:, _GAB - _GQKV:_GAB - _GQKV + LANES]
    gate = proj[:, _GG - _GQKV:_GG - _GQKV + W]
    yield
    g_all = -jnp.exp(alog) * _softplus(gab + dtb)
    beta_all = _sigmoid(gab)
    b_all = _chunk_cumsum(g_all, TB)
    b_rows = jnp.transpose(b_all)
    negmask = negmask_ref[...]
    eye = gmask_ref[0].astype(F32)
    heads = range(GDN_HEADS)

    def l2n(a):
        return a * lax.rsqrt(jnp.sum(a * a, axis=-1, keepdims=True) + NORM_EPS)

    q = [l2n(qkv[:, h * GDN_D:(h + 1) * GDN_D]) * (GDN_D ** -0.5) for h in heads]
    k = [l2n(qkv[:, W + h * GDN_D:W + (h + 1) * GDN_D]) for h in heads]
    v = [qkv[:, 2 * W + h * GDN_D:2 * W + (h + 1) * GDN_D] for h in heads]
    bcol = [b_all[:, h:h + 1] for h in heads]
    beta = [beta_all[:, GDN_HEADS + h:GDN_HEADS + h + 1] for h in heads]
    gam = [jnp.exp(bcol[h] - b_rows[h:h + 1, :] + negmask) for h in heads]
    kb = [k[h] * beta[h] for h in heads]
    yield
    kq = [_dot_nt(jnp.concatenate([kb[h], q[h]], axis=0), k[h]) for h in heads]
    yield
    lmat = [(kq[h][:TB] * gam[h]).astype(BF16) * gmask_ref[1] for h in heads]
    attn = [(kq[h][TB:] * gam[h]).astype(BF16) for h in heads]
    l8 = [lmat[h] * gmask_ref[2] for h in heads]
    yield
    p2 = [_dot(l8[h], l8[h]).astype(BF16) for h in heads]
    p4 = [_dot(p2[h], p2[h]).astype(BF16) for h in heads]
    yield
    x0 = [eye - l8[h].astype(F32) for h in heads]
    x1 = [x0[h] + _dot(x0[h], p2[h]) for h in heads]
    yield
    inv = [(x1[h] + _dot(x1[h], p4[h])).astype(BF16) for h in heads]
    yield
    t = [_dot(inv[h], lmat[h] * gmask_ref[3]).astype(BF16) for h in heads]
    yield
    inv = [inv[h] - _dot(t[h], inv[h]).astype(BF16) for h in heads]
    yield
    sb = 2 * SUBLANES
    for m in range(4, gmask_ref.shape[0]):
        odd = [slice(r, r + sb) for r in range(sb, TB, 2 * sb)]
        take = lambda a: jnp.concatenate([a[r] for r in odd], axis=0) if len(odd) > 1 else a[odd[0]]
        t = [_dot(take(inv[h]), lmat[h] * gmask_ref[m]).astype(BF16) for h in heads]
        yield
        new_rows = [take(inv[h]) - _dot(t[h], inv[h]).astype(BF16) for h in heads]
        yield
        rebuilt = []
        for h in heads:
            parts = []
            for n in range(TB // (2 * sb)):
                parts.append(inv[h][2 * n * sb:(2 * n + 1) * sb])
                parts.append(new_rows[h][n * sb:(n + 1) * sb])
            rebuilt.append(jnp.concatenate(parts, axis=0))
        inv = rebuilt
        sb *= 2
    eb = [jnp.exp(bcol[h]) for h in heads]
    s = [s_scr[h] for h in heads]
    zero = jnp.zeros((GDN_D, GDN_D), BF16)
    ks = [None] * GDN_HEADS
    for h0 in range(0, GDN_HEADS, 2):
        h1 = h0 + 1
        lhs = jnp.concatenate([
            jnp.concatenate([kb[h0] * eb[h0], kb[h1] * eb[h1]], axis=1),
            jnp.concatenate([q[h0] * eb[h0], q[h1] * eb[h1]], axis=1)], axis=0)
        s0, s1 = s[h0].astype(BF16), s[h1].astype(BF16)
        rhs = jnp.concatenate([jnp.concatenate([s0, zero], axis=1), jnp.concatenate([zero, s1], axis=1)], axis=0)
        pair = _dot(lhs, rhs)
        ks[h0], ks[h1] = pair[:, :GDN_D], pair[:, GDN_D:]
    yield
    v_new = [_dot(inv[h], v[h] * beta[h] - ks[h][:TB]) for h in heads]
    yield
    o = [ks[h][TB:] + _dot(attn[h], v_new[h]) for h in heads]
    yield
    for h in heads:
        blast = bcol[h][TB - 1:TB, :]
        s_scr[h] = s[h] * jnp.exp(blast) + _dot_tn(k[h] * jnp.exp(blast - bcol[h]), v_new[h])
    o = [o[h] * lax.rsqrt(jnp.mean(o[h] * o[h], axis=-1, keepdims=True) + NORM_EPS) for h in heads]
    return jnp.concatenate(o, axis=1) * normw * _silu(gate)


def _round_robin(gens):
    results = [None] * len(gens)
    live = list(range(len(gens)))
    while live:
        for i in list(live):
            try:
                next(gens[i])
            except StopIteration as stop:
                results[i] = stop.value
                live.remove(i)
    return results


def _mixer_kernel(x_ref, cos_ref, sin_ref, n1w_ref, win_ref, lb_ref, convw_ref, alog_ref, dtb_ref,
                  hgw_ref, gdw_ref, wout_ref, rdecay_ref, rdin_ref, rdout_ref, rwdec_ref, rbd_ref,
                  qmask_ref, vmask_ref, ones64_ref, negmask_ref, gmask_ref,
                  out_ref,
                  proj_scr, sret_scr, shg_scr, sgd_scr, carry_scr):
    @pl.when(pl.program_id(1) == 0)
    def _():
        sret_scr[...] = jnp.zeros_like(sret_scr)
        shg_scr[...] = jnp.zeros_like(shg_scr)
        sgd_scr[...] = jnp.zeros_like(sgd_scr)
        carry_scr[...] = jnp.zeros_like(carry_scr)

    x = x_ref[0]
    h = (x * lax.rsqrt(jnp.mean(x * x, axis=-1, keepdims=True) + NORM_EPS) * n1w_ref[...]).astype(BF16)

    def project(lo, hi):
        proj_scr[:, lo:hi] = jnp.dot(h, win_ref[:, lo:hi], preferred_element_type=F32)
        return proj_scr[:, lo:hi]

    o_gdn, o_hg, o_ret = _round_robin([
        _gdn(project, convw_ref[...], alog_ref[...], dtb_ref[...], gdw_ref[...], negmask_ref,
             gmask_ref, carry_scr, sgd_scr),
        _hgrn2(project, lb_ref[...], hgw_ref[...], ones64_ref, shg_scr),
        _retention(project, cos_ref[0], sin_ref[0], rdecay_ref, rdin_ref, rdout_ref, rwdec_ref,
                   rbd_ref, qmask_ref, vmask_ref, ones64_ref, sret_scr),
    ])
    mixed = jnp.concatenate([o_ret, o_hg, o_gdn], axis=1).astype(BF16)
    out_ref[0] = x + jnp.dot(mixed, wout_ref[...], preferred_element_type=F32)


def _gdn_tables():
    i = np.arange(TB)[:, None]
    j = np.arange(TB)[None, :]
    negmask = np.where(i >= j, 0.0, NEG_BIG).astype(np.float32)
    masks = [i == j, i > j, (i > j) & (i // SUBLANES == j // SUBLANES)]
    s = SUBLANES
    while s < TB:
        masks.append((i // (2 * s) == j // (2 * s)) & (i // s > j // s))
        s *= 2
    return jnp.asarray(negmask), jnp.asarray(np.stack(masks).astype(np.float32), dtype=BF16)


def _retention_tables():
    hh = np.arange(RET_HEADS, dtype=np.float32)
    log_gamma = jnp.log1p(-jnp.exp2(-5.0 - jnp.asarray(hh)))
    idx = jnp.arange(TB, dtype=F32)
    rel = idx[:, None] - idx[None, :]
    causal = rel >= 0
    decay = jnp.where(causal[None], jnp.exp(jnp.where(causal[None], rel[None] * log_gamma[:, None, None], 0.0)), 0.0)
    lane = np.arange(256)
    head_qk = (lane % LANES) // (RET_D // 2)
    head_v = lane // RET_D
    lg_lane = log_gamma[head_qk]
    din = jnp.exp((idx + 1.0)[:, None] * lg_lane[None, :])
    dout = jnp.exp((TB - 1.0 - idx)[:, None] * lg_lane[None, :])
    bd = (head_qk[:, None] == head_v[None, :]).astype(np.float32)
    wdec = jnp.exp(TB * lg_lane)[:, None] * bd
    qmask = np.stack([(head_qk == h) for h in range(RET_HEADS)]).astype(np.float32)[:, None, :]
    vmask = np.stack([(head_v == h) for h in range(RET_HEADS)]).astype(np.float32)[:, None, :]
    ones64 = (head_v[:, None] == head_v[None, :]).astype(np.float32)
    return (decay, din, dout, wdec, jnp.asarray(bd), jnp.asarray(qmask, dtype=BF16),
            jnp.asarray(vmask, dtype=BF16), jnp.asarray(ones64, dtype=BF16))


def _regroup_projection(w_in):
    w = w_in.astype(BF16)
    half = RET_D // 2
    pieces = []
    for base in (0, RET_HEADS * RET_D):
        for part in range(2):
            pieces += [w[..., base + h * RET_D + part * half:base + h * RET_D + (part + 1) * half]
                       for h in range(RET_HEADS)]
    pieces += [w[..., 512:3584], w[..., 3592:4104], w[..., 3584:3592]]
    pieces.append(jnp.zeros(w.shape[:-1] + (D_PROJ_PAD - w.shape[-1],), BF16))
    return jnp.concatenate(pieces, axis=-1)


def _const_spec(shape):
    nd = len(shape)
    return pl.BlockSpec(shape, lambda b, t: (0,) * nd)


def _layer_spec(stacked, layer):
    nd = stacked.ndim
    return pl.BlockSpec((None,) + stacked.shape[1:], lambda b, t: (layer,) + (0,) * (nd - 1))


def _mixer_layer(x, cos_t, sin_t, layer, stacked, tables):
    B, T, D = x.shape
    tok = lambda w: pl.BlockSpec((1, TB, w), lambda b, t: (b, t, 0))
    return pl.pallas_call(
        _mixer_kernel,
        grid=(B, T // TB),
        in_specs=([tok(D), tok(RET_D // 2), tok(RET_D // 2)] + [_layer_spec(a, layer) for a in stacked]
                  + [_const_spec(c.shape) for c in tables]),
        out_specs=tok(D),
        out_shape=jax.ShapeDtypeStruct((B, T, D), F32),
        scratch_shapes=[
            pltpu.VMEM((TB, D_PROJ_PAD), F32),
            pltpu.VMEM((256, 256), F32),
            pltpu.VMEM((256, 256), F32),
            pltpu.VMEM((GDN_HEADS, GDN_D, GDN_D), F32),
            pltpu.VMEM((SUBLANES, 3 * GDN_HEADS * GDN_D), F32),
        ],
        compiler_params=pltpu.CompilerParams(
            dimension_semantics=("parallel", "arbitrary"), vmem_limit_bytes=VMEM_LIMIT),
        name="mixer_layer",
    )(x, cos_t, sin_t, *stacked, *tables)


def _ffn_kernel(x_ref, n2w_ref, wup_ref, cw_ref, cb_ref, wd_ref, fw_ref, out_ref, carry_scr, act_scr, *,
                final_norm):
    @pl.when(pl.program_id(1) == 0)
    def _():
        carry_scr[...] = jnp.zeros_like(carry_scr)

    x = x_ref[0]
    h = (x * lax.rsqrt(jnp.mean(x * x, axis=-1, keepdims=True) + NORM_EPS) * n2w_ref[...]).astype(BF16)
    for c in range(D_FF // FF_TILE):
        cols = slice(c * FF_TILE, (c + 1) * FF_TILE)
        ys = []
        for off in (0, D_FF):
            wide = slice(off + c * FF_TILE, off + (c + 1) * FF_TILE)
            u = jnp.dot(h, wup_ref[:, wide], preferred_element_type=F32)
            ys.append(_causal_conv(u, carry_scr[:, wide], cw_ref[:, wide], FFN_CONV) + cb_ref[:, wide])
            carry_scr[:, wide] = u[FFN_TB - SUBLANES:FFN_TB]
        act_scr[:, cols] = _silu(ys[0].astype(BF16)) * ys[1].astype(BF16)
    o = x + jnp.dot(act_scr[...], wd_ref[...], preferred_element_type=F32)
    if final_norm:
        o = o * lax.rsqrt(jnp.mean(o * o, axis=-1, keepdims=True) + NORM_EPS) * fw_ref[...]
    out_ref[0] = o


def _ffn_layer(x, layer, n2w, wup, cw, cb, wd, fw, final_norm):
    B, T, D = x.shape
    stacked = [n2w, wup, cw, cb, wd]
    tok = pl.BlockSpec((1, FFN_TB, D), lambda b, t: (b, t, 0))
    return pl.pallas_call(
        functools.partial(_ffn_kernel, final_norm=final_norm),
        grid=(B, T // FFN_TB),
        in_specs=[tok] + [_layer_spec(a, layer) for a in stacked] + [_const_spec(fw.shape)],
        out_specs=tok,
        out_shape=jax.ShapeDtypeStruct((B, T, D), F32),
        scratch_shapes=[pltpu.VMEM((SUBLANES, 2 * D_FF), F32), pltpu.VMEM((FFN_TB, D_FF), BF16)],
        compiler_params=pltpu.CompilerParams(
            dimension_semantics=("parallel", "arbitrary"), vmem_limit_bytes=VMEM_LIMIT),
        name="ffn_layer",
    )(x, *stacked, fw)


def kernel(x, positions, norm1_w, w_in, hg_lb_logits, gdn_conv_w, gdn_A_log, gdn_dt_bias, hg_norm_w,
           gdn_norm_w, w_out, norm2_w, w_up, ffn_conv_w, ffn_conv_b, w_down, final_norm_w):
    depth = w_in.shape[0]
    assert x.shape[1] % TB == 0 and x.shape[1] % FFN_TB == 0 and x.shape[2] == D_MODEL
    p = jax.nn.softmax(hg_lb_logits.astype(F32), axis=0)
    lower_bounds = jnp.maximum(jnp.cumsum(p, axis=0) - p[0:1], 0.0)
    cos_t, sin_t = _rope_tables(positions)
    tables = _retention_tables() + _gdn_tables()
    rows = lambda v: v.astype(F32)[:, None, :]
    pad4 = lambda v: jnp.pad(v.astype(F32), ((0, 0), (0, LANES - GDN_HEADS)))[:, None, :]
    mixer_params = [
        rows(norm1_w), _regroup_projection(w_in), rows(lower_bounds), gdn_conv_w.astype(F32),
        pad4(gdn_A_log), pad4(gdn_dt_bias), rows(jnp.tile(hg_norm_w, (1, HG_HEADS))),
        rows(jnp.tile(gdn_norm_w, (1, GDN_HEADS))), w_out.astype(BF16)]
    ffn_params = [rows(norm2_w), w_up.astype(BF16), ffn_conv_w.astype(F32), rows(ffn_conv_b),
                  w_down.astype(BF16)]
    fw = final_norm_w.astype(F32)[None, :]
    for layer in range(depth):
        x = _mixer_layer(x, cos_t, sin_t, layer, mixer_params, tables)
        x = _ffn_layer(x, layer, *ffn_params, fw, final_norm=(layer == depth - 1))
    return x
```

```python
import functools

import numpy as np
import jax
import jax.numpy as jnp
from jax import lax
from jax.experimental import pallas as pl
from jax.experimental.pallas import tpu as pltpu

F32 = jnp.float32
BF16 = jnp.bfloat16

D_MODEL = 1024
RET_HEADS = 4
RET_D = 64
HG_HEADS = 4
HG_D = 64
GDN_HEADS = 4
GDN_D = 128
GDN_CONV = 4
FFN_CONV = 3
D_FF = 2816
ROPE_BASE = 10000.0
NORM_EPS = 1e-6
EXP_CLIP = 80.0
NEG_BIG = -1e30

LANES = 128
SUBLANES = 8
TB = 256
HG_CHUNK = 16
FF_TILE = 256
FFN_TB = 512
VMEM_LIMIT = 56 * 1024 * 1024

RET_W = RET_HEADS * RET_D
HG_W = HG_HEADS * HG_D
GDN_W = GDN_HEADS * GDN_D
assert RET_W == HG_W == 2 * LANES and RET_D == HG_D

_RQ = 0
_HQ = 4 * RET_W
_GQKV = _HQ + 4 * HG_W
_GAB = _GQKV + 3 * GDN_W
_GG = _GAB + 2 * GDN_HEADS
D_PROJ = _GG + GDN_W
D_PROJ_PAD = -(-D_PROJ // LANES) * LANES


def _dot(a, b):
    return jnp.dot(a.astype(BF16), b.astype(BF16), preferred_element_type=F32)


def _dot_nt(a, b):
    return lax.dot_general(a.astype(BF16), b.astype(BF16), (((1,), (1,)), ((), ())),
                           preferred_element_type=F32)


def _dot_tn(a, b):
    return lax.dot_general(a.astype(BF16), b.astype(BF16), (((0,), (0,)), ((), ())),
                           preferred_element_type=F32)


def _sigmoid(x):
    return 1.0 / (1.0 + jnp.exp(-x))


def _silu(x):
    return x * _sigmoid(x)


def _softplus(x):
    return jnp.maximum(x, 0.0) + jnp.log(1.0 + jnp.exp(-jnp.abs(x)))


def _chunk_cumsum(x, chunk):
    row = lax.broadcasted_iota(jnp.int32, (x.shape[0], 1), 0) % chunk
    s = 1
    while s < chunk:
        x = x + jnp.where(row >= s, pltpu.roll(x, s, axis=0), 0.0)
        s *= 2
    return x


def _causal_conv(u, prev, w, width):
    acc = u * w[width - 1:width]
    row = lax.broadcasted_iota(jnp.int32, (SUBLANES, 1), 0)
    tiles = [prev] + [u[i:i + SUBLANES] for i in range(0, u.shape[0], SUBLANES)]
    for s in range(1, width):
        rot = [pltpu.roll(t, s, axis=0) for t in tiles]
        shifted = jnp.concatenate([jnp.where(row < s, rot[i], rot[i + 1]) for i in range(len(tiles) - 1)], axis=0)
        acc = acc + shifted * w[width - 1 - s:width - s]
    return acc


def _rope_kernel(pos_ref, invf_ref, cos_ref, sin_ref):
    ang = pos_ref[0] * invf_ref[...]
    cos_ref[0] = jnp.cos(ang)
    sin_ref[0] = jnp.sin(ang)


def _rope_tables(positions):
    B, T = positions.shape
    half = RET_D // 2
    rep = LANES // half
    inv_freq = ROPE_BASE ** (-jnp.arange(half, dtype=F32) / half)
    invf = jnp.tile(inv_freq, rep)[None, :]
    pos = jnp.repeat(positions.astype(F32).reshape(B, T // rep, rep), half, axis=-1)
    rows = T // rep
    tb = 512 if rows % 512 == 0 else rows
    spec = pl.BlockSpec((1, tb, LANES), lambda b, t: (b, t, 0))
    cos_t, sin_t = pl.pallas_call(
        _rope_kernel,
        grid=(B, rows // tb),
        in_specs=[spec, pl.BlockSpec((1, LANES), lambda b, t: (0, 0))],
        out_specs=[spec, spec],
        out_shape=[jax.ShapeDtypeStruct((B, rows, LANES), F32)] * 2,
        name="rope_tables",
    )(pos, invf)
    return cos_t.reshape(B, T, half), sin_t.reshape(B, T, half)


def _retention(project, cos, sin, decay_ref, din_ref, dout_ref, wdec_ref, bd_ref, qmask_ref,
               vmask_ref, ones64_ref, s_scr):
    proj = project(_RQ, _HQ)
    yield
    q, k, v, g = (proj[:, i * RET_W:(i + 1) * RET_W] for i in range(4))

    cos =jnp.concatenate([cos] * RET_HEADS, axis=1)
    sin = jnp.concatenate([sin] * RET_HEADS, axis=1)

    def rot(a):
        a1, a2 = a[:, :LANES], a[:, LANES:]
        return jnp.concatenate([a1 * cos - a2 * sin, a1 * sin + a2 * cos], axis=1)

    qr = rot(q)
    kr = rot(k) * (RET_D ** -0.5)
    qr_b, kr_b, v_b = qr.astype(BF16), kr.astype(BF16), v.astype(BF16)
    yield
    intra = None
    for h in range(RET_HEADS):
        a = _dot_nt(qr_b * qmask_ref[h], kr_b) * decay_ref[h]
        part = _dot(a, v_b * vmask_ref[h])
        intra = part if intra is None else intra + part
        yield
    s = s_scr[...]
    inter = _dot(qr * din_ref[...], s)
    u = _dot_tn(kr * dout_ref[...], v)
    s_scr[...] = wdec_ref[...] * s + bd_ref[...] * u
    yield
    o = intra + inter
    ss = _dot(o * o, ones64_ref[...])
    return o * lax.rsqrt(ss * (1.0 / RET_D) + NORM_EPS) * _silu(g)


def _hgrn2(project, lb, normw, ones64_ref, st_scr):
    proj = project(_HQ, _GQKV)
    yield
    hq, z, vi, gate = (proj[:, i * HG_W:(i + 1) * HG_W] for i in range(4))
    log_sig = jnp.minimum(z, 0.0) - jnp.log(1.0 + jnp.exp(-jnp.abs(z)))
    log_f = log_sig + jnp.log(1.0 + lb * jnp.exp(jnp.minimum(-z, EXP_CLIP)))
    q = _silu(hq)
    k = (1.0 - lb) * _sigmoid(-z)
    b = _chunk_cumsum(log_f, HG_CHUNK)
    ones64 = ones64_ref[...]
    ones64_f = ones64.astype(F32)
    C = HG_CHUNK
    row8 = lax.broadcasted_iota(jnp.int32, (SUBLANES, 1), 0)
    st = st_scr[...]
    outs = []
    yield
    for c in range(TB // C):
        r0 = c * C
        qc, kc, vc, bc = q[r0:r0 + C], k[r0:r0 + C], vi[r0:r0 + C], b[r0:r0 + C]
        blast = bc[C - 1:C, :]
        inter = _dot_nt(qc * jnp.exp(bc), st)
        u = _dot_tn(vc, kc * jnp.exp(blast - bc))
        st = jnp.exp(blast) * st + ones64_f * u
        parts = []
        for tile in range(C // SUBLANES):
            lo = tile * SUBLANES
            qt, bt = qc[lo:lo + SUBLANES], bc[lo:lo + SUBLANES]
            for j in range(lo + SUBLANES):
                diff = bt - bc[j:j + 1, :]
                if j > lo:
                    diff = jnp.where(row8 >= j - lo, diff, NEG_BIG)
                parts.append(qt * kc[j:j + 1, :] * jnp.exp(diff))
        p = jnp.concatenate(parts, axis=0).astype(BF16)
        r = jnp.dot(p, ones64, preferred_element_type=F32)
        accs = []
        off = 0
        for tile in range(C // SUBLANES):
            lo = tile * SUBLANES
            acc = inter[lo:lo + SUBLANES]
            for j in range(lo + SUBLANES):
                acc = acc + r[off:off + SUBLANES] * vc[j:j + 1, :]
                off += SUBLANES
            accs.append(acc)
        outs.extend(accs)
        yield
    st_scr[...] = st
    o = jnp.concatenate(outs, axis=0)
    ss = _dot(o * o, ones64)
    return o * lax.rsqrt(ss * (1.0 / HG_D) + NORM_EPS) * normw * _silu(gate)


def _gdn(project, convw, alog, dtb, normw, negmask_ref, gmask_ref, carry_scr, s_scr):
    proj = project(_GQKV, D_PROJ_PAD)
    yield
    raw = proj[:, 0:3 * GDN_W]
    qkv = _silu(_causal_conv(raw, carry_scr[...], convw, GDN_CONV))
    carry_scr[...] = raw[TB - SUBLANES:TB]
    W = GDN_W
    gab = proj[:, _GAB - _GQKV:_GAB - _GQKV + LANES]
    gate = proj[:, _GG - _GQKV:_GG - _GQKV + W]
    yield
    g_all = -jnp.exp(alog) * _softplus(gab + dtb)
    beta_all = _sigmoid(gab)
    b_all = _chunk_cumsum(g_all, TB)
    b_rows = jnp.transpose(b_all)
    negmask = negmask_ref[...]
    eye = gmask_ref[0].astype(F32)
    heads = range(GDN_HEADS)

    def l2n(a):
        return a * lax.rsqrt(jnp.sum(a * a, axis=-1, keepdims=True) + NORM_EPS)

    q = [l2n(qkv[:, h * GDN_D:(h + 1) * GDN_D]) * (GDN_D ** -0.5) for h in heads]
    k = [l2n(qkv[:, W + h * GDN_D:W + (h + 1) * GDN_D]) for h in heads]
    v = [qkv[:, 2 * W + h * GDN_D:2 * W + (h + 1) * GDN_D] for h in heads]
    bcol = [b_all[:, h:h + 1] for h in heads]
    beta = [beta_all[:, GDN_HEADS + h:GDN_HEADS + h + 1] for h in heads]
    gam = [jnp.exp(bcol[h] - b_rows[h:h + 1, :] + negmask) for h in heads]
    kb = [k[h] * beta[h] for h in heads]
    yield
    kq = [_dot_nt(jnp.concatenate([kb[h], q[h]], axis=0), k[h]) for h in heads]
    yield
    lmat = [(kq[h][:TB] * gam[h]).astype(BF16) * gmask_ref[1] for h in heads]
    attn = [(kq[h][TB:] * gam[h]).astype(BF16) for h in heads]
    l8 = [lmat[h] * gmask_ref[2] for h in heads]
    yield
    p2 = [_dot(l8[h], l8[h]).astype(BF16) for h in heads]
    p4 = [_dot(p2[h], p2[h]).astype(BF16) for h in heads]
    yield
    x0 = [eye - l8[h].astype(F32) for h in heads]
    x1 = [x0[h] + _dot(x0[h], p2[h]) for h in heads]
    yield
    inv = [(x1[h] + _dot(x1[h], p4[h])).astype(BF16) for h in heads]
    yield
    t = [_dot(inv[h], lmat[h] * gmask_ref[3]).astype(BF16) for h in heads]
    yield
    inv = [inv[h] - _dot(t[h], inv[h]).astype(BF16) for h in heads]
    yield
    sb = 2 * SUBLANES
    for m in range(4, gmask_ref.shape[0]):
        odd = [slice(r, r + sb) for r in range(sb, TB, 2 * sb)]
        take = lambda a: jnp.concatenate([a[r] for r in odd], axis=0) if len(odd) > 1 else a[odd[0]]
        t = [_dot(take(inv[h]), lmat[h] * gmask_ref[m]).astype(BF16) for h in heads]
        yield
        new_rows = [take(inv[h]) - _dot(t[h], inv[h]).astype(BF16) for h in heads]
        yield
        rebuilt = []
        for h in heads:
            parts = []
            for n in range(TB // (2 * sb)):
                parts.append(inv[h][2 * n * sb:(2 * n + 1) * sb])
                parts.append(new_rows[h][n * sb:(n + 1) * sb])
            rebuilt.append(jnp.concatenate(parts, axis=0))
        inv = rebuilt
        sb *= 2
    eb = [jnp.exp(bcol[h]) for h in heads]
    s = [s_scr[h] for h in heads]
    zero = jnp.zeros((GDN_D, GDN_D), BF16)
    ks = [None] * GDN_HEADS
    for h0 in range(0, GDN_HEADS, 2):
        h1 = h0 + 1
        lhs = jnp.concatenate([
            jnp.concatenate([kb[h0] * eb[h0], kb[h1] * eb[h1]], axis=1),
            jnp.concatenate([q[h0] * eb[h0], q[h1] * eb[h1]], axis=1)], axis=0)
        s0, s1 = s[h0].astype(BF16), s[h1].astype(BF16)
        rhs = jnp.concatenate([jnp.concatenate([s0, zero], axis=1), jnp.concatenate([zero, s1], axis=1)], axis=0)
        pair = _dot(lhs, rhs)
        ks[h0], ks[h1] = pair[:, :GDN_D], pair[:, GDN_D:]
    yield
    v_new = [_dot(inv[h], v[h] * beta[h] - ks[h][:TB]) for h in heads]
    yield
    o = [ks[h][TB:] + _dot(attn[h], v_new[h]) for h in heads]
    yield
    for h in heads:
        blast = bcol[h][TB - 1:TB, :]
        s_scr[h] = s[h] * jnp.exp(blast) + _dot_tn(k[h] * jnp.exp(blast - bcol[h]), v_new[h])
    o = [o[h] * lax.rsqrt(jnp.mean(o[h] * o[h], axis=-1, keepdims=True) + NORM_EPS) for h in heads]
    return jnp.concatenate(o, axis=1) * normw * _silu(gate)


def _round_robin(gens):
    results = [None] * len(gens)
    live = list(range(len(gens)))
    while live:
        for i in list(live):
            try:
                next(gens[i])
            except StopIteration as stop:
                results[i] = stop.value
                live.remove(i)
    return results


def _mixer_kernel(x_ref, cos_ref, sin_ref, n1w_ref, win_ref, lb_ref, convw_ref, alog_ref, dtb_ref,
                  hgw_ref, gdw_ref, wout_ref, rdecay_ref, rdin_ref, rdout_ref, rwdec_ref, rbd_ref,
                  qmask_ref, vmask_ref, ones64_ref, negmask_ref, gmask_ref,
                  out_ref,
                  proj_scr, sret_scr, shg_scr, sgd_scr, carry_scr):
    @pl.when(pl.program_id(1) == 0)
    def _():
        sret_scr[...] = jnp.zeros_like(sret_scr)
        shg_scr[...] = jnp.zeros_like(shg_scr)
        sgd_scr[...] = jnp.zeros_like(sgd_scr)
        carry_scr[...] = jnp.zeros_like(carry_scr)

    x = x_ref[0]
    h = (x * lax.rsqrt(jnp.mean(x * x, axis=-1, keepdims=True) + NORM_EPS) * n1w_ref[...]).astype(BF16)

    def project(lo, hi):
        proj_scr[:, lo:hi] = jnp.dot(h, win_ref[:, lo:hi], preferred_element_type=F32)
        return proj_scr[:, lo:hi]

    o_gdn, o_hg, o_ret = _round_robin([
        _gdn(project, convw_ref[...], alog_ref[...], dtb_ref[...], gdw_ref[...], negmask_ref,
             gmask_ref, carry_scr, sgd_scr),
        _hgrn2(project, lb_ref[...], hgw_ref[...], ones64_ref, shg_scr),
        _retention(project, cos_ref[0], sin_ref[0], rdecay_ref, rdin_ref, rdout_ref, rwdec_ref,
                   rbd_ref, qmask_ref, vmask_ref, ones64_ref, sret_scr),
    ])
    mixed = jnp.concatenate([o_ret, o_hg, o_gdn], axis=1).astype(BF16)
    out_ref[0] = x + jnp.dot(mixed, wout_ref[...], preferred_element_type=F32)


def _gdn_tables():
    i = np.arange(TB)[:, None]
    j = np.arange(TB)[None, :]
    negmask = np.where(i >= j, 0.0, NEG_BIG).astype(np.float32)
    masks = [i == j, i > j, (i > j) & (i // SUBLANES == j // SUBLANES)]
    s = SUBLANES
    while s < TB:
        masks.append((i // (2 * s) == j // (2 * s)) & (i // s > j // s))
        s *= 2
    return jnp.asarray(negmask), jnp.asarray(np.stack(masks).astype(np.float32), dtype=BF16)


def _retention_tables():
    hh = np.arange(RET_HEADS, dtype=np.float32)
    log_gamma = jnp.log1p(-jnp.exp2(-5.0 - jnp.asarray(hh)))
    idx = jnp.arange(TB, dtype=F32)
    rel = idx[:, None] - idx[None, :]
    causal = rel >= 0
    decay = jnp.where(causal[None], jnp.exp(jnp.where(causal[None], rel[None] * log_gamma[:, None, None], 0.0)), 0.0)
    lane = np.arange(RET_W)
    head_qk = (lane % LANES) // (RET_D // 2)
    head_v = lane // RET_D
    lg_lane = log_gamma[head_qk]
    din = jnp.exp((idx + 1.0)[:, None] * lg_lane[None, :])
    dout = jnp.exp((TB - 1.0 - idx)[:, None] * lg_lane[None, :])
    bd = (head_qk[:, None] == head_v[None, :]).astype(np.float32)
    wdec = jnp.exp(TB * lg_lane)[:, None] * bd
    qmask = np.stack([(head_qk == h) for h in range(RET_HEADS)]).astype(np.float32)[:, None, :]
    vmask = np.stack([(head_v == h) for h in range(RET_HEADS)]).astype(np.float32)[:, None, :]
    ones64 = (head_v[:, None] == head_v[None, :]).astype(np.float32)
    return (decay, din, dout, wdec, jnp.asarray(bd), jnp.asarray(qmask, dtype=BF16),
            jnp.asarray(vmask, dtype=BF16), jnp.asarray(ones64, dtype=BF16))


def _regroup_projection(w_in):
    assert w_in.shape[-1] == D_PROJ
    w = w_in.astype(BF16)
    half = RET_D // 2
    pieces = []
    for base in (0, RET_W):
        for part in range(2):
            pieces += [w[..., base + h * RET_D + part * half:base + h * RET_D + (part + 1) * half]
                       for h in range(RET_HEADS)]
    pieces.append(w[..., 2 * RET_W:])
    pieces.append(jnp.zeros(w.shape[:-1] + (D_PROJ_PAD - D_PROJ,), BF16))
    return jnp.concatenate(pieces, axis=-1)


def _const_spec(shape):
    nd = len(shape)
    return pl.BlockSpec(shape, lambda b, t: (0,) * nd)


def _layer_spec(stacked, layer):
    nd = stacked.ndim
    return pl.BlockSpec((None,) + stacked.shape[1:], lambda b, t: (layer,) + (0,) * (nd - 1))


def _mixer_layer(x, cos_t, sin_t, layer, stacked, tables):
    B, T, D = x.shape
    tok = lambda w: pl.BlockSpec((1, TB, w), lambda b, t: (b, t, 0))
    return pl.pallas_call(
        _mixer_kernel,
        grid=(B, T // TB),
        in_specs=([tok(D), tok(RET_D // 2), tok(RET_D // 2)] + [_layer_spec(a, layer) for a in stacked]
                  + [_const_spec(c.shape) for c in tables]),
        out_specs=tok(D),
        out_shape=jax.ShapeDtypeStruct((B, T, D), F32),
        scratch_shapes=[
            pltpu.VMEM((TB, D_PROJ_PAD), F32),
            pltpu.VMEM((RET_W, RET_W), F32),
            pltpu.VMEM((HG_W, HG_W), F32),
            pltpu.VMEM((GDN_HEADS, GDN_D, GDN_D), F32),
            pltpu.VMEM((SUBLANES, 3 * GDN_W), F32),
        ],
        compiler_params=pltpu.CompilerParams(
            dimension_semantics=("parallel", "arbitrary"), vmem_limit_bytes=VMEM_LIMIT),
        name="mixer_layer",
    )(x, cos_t, sin_t, *stacked, *tables)


def _ffn_kernel(x_ref, n2w_ref, wup_ref, cw_ref, cb_ref, wd_ref, fw_ref, out_ref, carry_scr, act_scr, *,
                final_norm):
    @pl.when(pl.program_id(1) == 0)
    def _():
        carry_scr[...] = jnp.zeros_like(carry_scr)

    x = x_ref[0]
    h = (x * lax.rsqrt(jnp.mean(x * x, axis=-1, keepdims=True) + NORM_EPS) * n2w_ref[...]).astype(BF16)
    for c in range(D_FF // FF_TILE):
        cols = slice(c * FF_TILE, (c + 1) * FF_TILE)
        ys = []
        for off in (0, D_FF):
            wide = slice(off + c * FF_TILE, off + (c + 1) * FF_TILE)
            u = jnp.dot(h, wup_ref[:, wide], preferred_element_type=F32)
            ys.append(_causal_conv(u, carry_scr[:, wide], cw_ref[:, wide], FFN_CONV) + cb_ref[:, wide])
            carry_scr[:, wide] = u[FFN_TB - SUBLANES:FFN_TB]
        act_scr[:, cols] = _silu(ys[0].astype(BF16)) * ys[1].astype(BF16)
    o = x + jnp.dot(act_scr[...], wd_ref[...], preferred_element_type=F32)
    if final_norm:
        o = o * lax.rsqrt(jnp.mean(o * o, axis=-1, keepdims=True) + NORM_EPS) * fw_ref[...]
    out_ref[0] = o


def _ffn_layer(x, layer, n2w, wup, cw, cb, wd, fw, final_norm):
    B, T, D = x.shape
    stacked = [n2w, wup, cw, cb, wd]
    tok = pl.BlockSpec((1, FFN_TB, D), lambda b, t: (b, t, 0))
    return pl.pallas_call(
        functools.partial(_ffn_kernel, final_norm=final_norm),
        grid=(B, T // FFN_TB),
        in_specs=[tok] + [_layer_spec(a, layer) for a in stacked] + [_const_spec(fw.shape)],
        out_specs=tok,
        out_shape=jax.ShapeDtypeStruct((B, T, D), F32),
        scratch_shapes=[pltpu.VMEM((SUBLANES, 2 * D_FF), F32), pltpu.VMEM((FFN_TB, D_FF), BF16)],
        compiler_params=pltpu.CompilerParams(
            dimension_semantics=("parallel", "arbitrary"), vmem_limit_bytes=VMEM_LIMIT),
        name="ffn_layer",
    )(x, *stacked, fw)


def kernel(x, positions, norm1_w, w_in, hg_lb_logits, gdn_conv_w, gdn_A_log, gdn_dt_bias, hg_norm_w,
           gdn_norm_w, w_out, norm2_w, w_up, ffn_conv_w, ffn_conv_b, w_down, final_norm_w):
    depth = w_in.shape[0]
    assert x.shape[1] % TB == 0 and x.shape[1] % FFN_TB == 0 and x.shape[2] == D_MODEL
    p = jax.nn.softmax(hg_lb_logits.astype(F32), axis=0)
    lower_bounds = jnp.maximum(jnp.cumsum(p, axis=0) - p[0:1], 0.0)
    cos_t, sin_t = _rope_tables(positions)
    tables = _retention_tables() + _gdn_tables()
    rows = lambda v: v.astype(F32)[:, None, :]
    pad4 = lambda v: jnp.pad(v.astype(F32), ((0, 0), (0, LANES - GDN_HEADS)))[:, None, :]
    mixer_params = [
        rows(norm1_w), _regroup_projection(w_in), rows(lower_bounds), gdn_conv_w.astype(F32),
        pad4(gdn_A_log), pad4(gdn_dt_bias), rows(jnp.tile(hg_norm_w, (1, HG_HEADS))),
        rows(jnp.tile(gdn_norm_w, (1, GDN_HEADS))), w_out.astype(BF16)]
    ffn_params = [rows(norm2_w), w_up.astype(BF16), ffn_conv_w.astype(F32), rows(ffn_conv_b),
                  w_down.astype(BF16)]
    fw = final_norm_w.astype(F32)[None, :]
    for layer in range(depth):
        x = _mixer_layer(x, cos_t, sin_t, layer, mixer_params, tables)
        x = _ffn_layer(x, layer, *ffn_params, fw, final_norm=(layer == depth - 1))
    return x
```

```python
import functools

import numpy as np
import jax
import jax.numpy as jnp
from jax import lax
from jax.experimental import pallas as pl
from jax.experimental.pallas import tpu as pltpu

F32 = jnp.float32
BF16 = jnp.bfloat16

D_MODEL = 1024
RET_HEADS = 4
RET_D = 64
HG_HEADS = 4
HG_D = 64
GDN_HEADS = 4
GDN_D = 128
GDN_CONV = 4
FFN_CONV = 3
D_FF = 2816
ROPE_BASE = 10000.0
NORM_EPS = 1e-6
EXP_CLIP = 80.0
NEG_BIG = -1e30

LANES = 128
SUBLANES = 8
TB = 256
HG_CHUNK = 16
FF_TILE = 256
FFN_TB = 512
ROPE_ROWS = 512
REGROUP_ROWS = 256
VMEM_LIMIT = 56 * 1024 * 1024

RET_W = RET_HEADS * RET_D
HG_W = HG_HEADS * HG_D
GDN_W = GDN_HEADS * GDN_D
assert RET_W == HG_W == 2 * LANES and RET_D == HG_D

_RQ = 0
_HQ = 4 * RET_W
_GQKV = _HQ + 4 * HG_W
_GG = _GQKV + 3 * GDN_W
_GAB = _GG + GDN_W
D_PROJ = _GAB + 2 * GDN_HEADS
D_PROJ_PAD = -(-D_PROJ // LANES) * LANES


def _dot(a, b):
    return jnp.dot(a.astype(BF16), b.astype(BF16), preferred_element_type=F32)


def _dot_nt(a, b):
    return lax.dot_general(a.astype(BF16), b.astype(BF16), (((1,), (1,)), ((), ())),
                           preferred_element_type=F32)


def _dot_tn(a, b):
    return lax.dot_general(a.astype(BF16), b.astype(BF16), (((0,), (0,)), ((), ())),
                           preferred_element_type=F32)


def _sigmoid(x):
    return 1.0 / (1.0 + jnp.exp(-x))


def _silu(x):
    return x * _sigmoid(x)


def _softplus(x):
    return jnp.maximum(x, 0.0) + jnp.log(1.0 + jnp.exp(-jnp.abs(x)))


def _chunk_cumsum(x, chunk):
    row = lax.broadcasted_iota(jnp.int32, (x.shape[0], 1), 0) % chunk
    s = 1
    while s < chunk:
        x = x + jnp.where(row >= s, pltpu.roll(x, s, axis=0), 0.0)
        s *= 2
    return x


def _causal_conv(u, prev, w, width):
    acc = u * w[width - 1:width]
    row = lax.broadcasted_iota(jnp.int32, (SUBLANES, 1), 0)
    tiles = [prev] + [u[i:i + SUBLANES] for i in range(0, u.shape[0], SUBLANES)]
    for s in range(1, width):
        rot = [pltpu.roll(t, s, axis=0) for t in tiles]
        shifted = jnp.concatenate([jnp.where(row < s, rot[i], rot[i + 1]) for i in range(len(tiles) - 1)], axis=0)
        acc = acc + shifted * w[width - 1 - s:width - s]
    return acc


def _rope_kernel(pos_ref, invf_ref, cos_ref, sin_ref):
    ang = pos_ref[0] * invf_ref[...]
    cos, sin = jnp.cos(ang), jnp.sin(ang)
    rows, half = ang.shape[0], RET_D // 2
    for r in range(LANES // half):
        cos_ref[0, r * rows:(r + 1) * rows, :] = cos[:, r * half:(r + 1) * half]
        sin_ref[0, r * rows:(r + 1) * rows, :] = sin[:, r * half:(r + 1) * half]


def _rope_tables(positions):
    B, T = positions.shape
    half = RET_D // 2
    rep = LANES // half
    inv_freq = ROPE_BASE ** (-jnp.arange(half, dtype=F32) / half)
    invf = jnp.tile(inv_freq, rep)[None, :]
    rows = ROPE_ROWS if T % (rep * ROPE_ROWS) == 0 else T // rep
    pos = positions.astype(F32).reshape(B, T // (rep * rows), rep, rows)
    pos = jnp.repeat(jnp.swapaxes(pos, 2, 3), half, axis=-1).reshape(B, T // rep, LANES)
    table = pl.BlockSpec((1, rep * rows, half), lambda b, t: (b, t, 0))
    return pl.pallas_call(
        _rope_kernel,
        grid=(B, T // (rep * rows)),
        in_specs=[pl.BlockSpec((1, rows, LANES), lambda b, t: (b, t, 0)),
                  pl.BlockSpec((1, LANES), lambda b, t: (0, 0))],
        out_specs=[table, table],
        out_shape=[jax.ShapeDtypeStruct((B, T, half), F32)] * 2,
        name="rope_tables",
    )(pos, invf)


def _retention(project, cos, sin, decay_ref, din_ref, dout_ref, wdec_ref, bd_ref, qmask_ref,
               vmask_ref, ones64_ref, s_scr):
    proj = project(_RQ, _HQ)
    yield
    q, k, v, g = (proj[:, i * RET_W:(i + 1) * RET_W] for i in range(4))

    cos =jnp.concatenate([cos] * RET_HEADS, axis=1)
    sin = jnp.concatenate([sin] * RET_HEADS, axis=1)

    def rot(a):
        a1, a2 = a[:, :LANES], a[:, LANES:]
        return jnp.concatenate([a1 * cos - a2 * sin, a1 * sin + a2 * cos], axis=1)

    qr = rot(q)
    kr = rot(k) * (RET_D ** -0.5)
    qr_b, kr_b, v_b = qr.astype(BF16), kr.astype(BF16), v.astype(BF16)
    yield
    intra = None
    for h in range(RET_HEADS):
        a = _dot_nt(qr_b * qmask_ref[h], kr_b) * decay_ref[h]
        part = _dot(a, v_b * vmask_ref[h])
        intra = part if intra is None else intra + part
        yield
    s = s_scr[...]
    inter = _dot(qr * din_ref[...], s)
    u = _dot_tn(kr * dout_ref[...], v)
    s_scr[...] = wdec_ref[...] * s + bd_ref[...] * u
    yield
    o = intra + inter
    ss = _dot(o * o, ones64_ref[...])
    return o * lax.rsqrt(ss * (1.0 / RET_D) + NORM_EPS) * _silu(g)


def _hgrn2(project, lb, normw, ones64_ref, st_scr):
    proj = project(_HQ, _GQKV)
    yield
    hq, z, vi, gate = (proj[:, i * HG_W:(i + 1) * HG_W] for i in range(4))
    log_sig = jnp.minimum(z, 0.0) - jnp.log(1.0 + jnp.exp(-jnp.abs(z)))
    log_f = log_sig + jnp.log(1.0 + lb * jnp.exp(jnp.minimum(-z, EXP_CLIP)))
    q = _silu(hq)
    k = (1.0 - lb) * _sigmoid(-z)
    b = _chunk_cumsum(log_f, HG_CHUNK)
    ones64 = ones64_ref[...]
    ones64_f = ones64.astype(F32)
    C = HG_CHUNK
    row8 = lax.broadcasted_iota(jnp.int32, (SUBLANES, 1), 0)
    st = st_scr[...]
    outs = []
    yield
    for c in range(TB // C):
        r0 = c * C
        qc, kc, vc, bc = q[r0:r0 + C], k[r0:r0 + C], vi[r0:r0 + C], b[r0:r0 + C]
        blast = bc[C - 1:C, :]
        inter = _dot_nt(qc * jnp.exp(bc), st)
        u = _dot_tn(vc, kc * jnp.exp(blast - bc))
        st = jnp.exp(blast) * st + ones64_f * u
        parts = []
        for tile in range(C // SUBLANES):
            lo = tile * SUBLANES
            qt, bt = qc[lo:lo + SUBLANES], bc[lo:lo + SUBLANES]
            for j in range(lo + SUBLANES):
                diff = bt - bc[j:j + 1, :]
                if j > lo:
                    diff = jnp.where(row8 >= j - lo, diff, NEG_BIG)
                parts.append(qt * kc[j:j + 1, :] * jnp.exp(diff))
        p = jnp.concatenate(parts, axis=0).astype(BF16)
        r = jnp.dot(p, ones64, preferred_element_type=F32)
        accs = []
        off = 0
        for tile in range(C // SUBLANES):
            lo = tile * SUBLANES
            acc = inter[lo:lo + SUBLANES]
            for j in range(lo + SUBLANES):
                acc = acc + r[off:off + SUBLANES] * vc[j:j + 1, :]
                off += SUBLANES
            accs.append(acc)
        outs.extend(accs)
        yield
    st_scr[...] = st
    o = jnp.concatenate(outs, axis=0)
    ss = _dot(o * o, ones64)
    return o * lax.rsqrt(ss * (1.0 / HG_D) + NORM_EPS) * normw * _silu(gate)


def _gdn(project, convw, alog, dtb, normw, negmask_ref, gmask_ref, carry_scr, s_scr):
    proj = project(_GQKV, D_PROJ_PAD)
    yield
    raw = proj[:, 0:3 * GDN_W]
    qkv = _silu(_causal_conv(raw, carry_scr[...], convw, GDN_CONV))
    carry_scr[...] = raw[TB - SUBLANES:TB]
    W = GDN_W
    gab = proj[:, _GAB - _GQKV:_GAB - _GQKV + LANES]
    gate = proj[:, _GG - _GQKV:_GG - _GQKV + W]
    yield
    g_all = -jnp.exp(alog) * _softplus(gab + dtb)
    beta_all = _sigmoid(gab)
    b_all = _chunk_cumsum(g_all, TB)
    b_rows = jnp.transpose(b_all)
    negmask = negmask_ref[...]
    eye = gmask_ref[0].astype(F32)
    heads = range(GDN_HEADS)

    def l2n(a):
        return a * lax.rsqrt(jnp.sum(a * a, axis=-1, keepdims=True) + NORM_EPS)

    q = [l2n(qkv[:, h * GDN_D:(h + 1) * GDN_D]) * (GDN_D ** -0.5) for h in heads]
    k = [l2n(qkv[:, W + h * GDN_D:W + (h + 1) * GDN_D]) for h in heads]
    v = [qkv[:, 2 * W + h * GDN_D:2 * W + (h + 1) * GDN_D] for h in heads]
    bcol = [b_all[:, h:h + 1] for h in heads]
    beta = [beta_all[:, GDN_HEADS + h:GDN_HEADS + h + 1] for h in heads]
    gam = [jnp.exp(bcol[h] - b_rows[h:h + 1, :] + negmask) for h in heads]
    kb = [k[h] * beta[h] for h in heads]
    yield
    kq = [_dot_nt(jnp.concatenate([kb[h], q[h]], axis=0), k[h]) for h in heads]
    yield
    lmat = [(kq[h][:TB] * gam[h]).astype(BF16) * gmask_ref[1] for h in heads]
    attn = [(kq[h][TB:] * gam[h]).astype(BF16) for h in heads]
    l8 = [lmat[h] * gmask_ref[2] for h in heads]
    yield
    p2 = [_dot(l8[h], l8[h]).astype(BF16) for h in heads]
    p4 = [_dot(p2[h], p2[h]).astype(BF16) for h in heads]
    yield
    x0 = [eye - l8[h].astype(F32) for h in heads]
    x1 = [x0[h] + _dot(x0[h], p2[h]) for h in heads]
    yield
    inv = [(x1[h] + _dot(x1[h], p4[h])).astype(BF16) for h in heads]
    yield
    t = [_dot(inv[h], lmat[h] * gmask_ref[3]).astype(BF16) for h in heads]
    yield
    inv = [inv[h] - _dot(t[h], inv[h]).astype(BF16) for h in heads]
    yield
    sb = 2 * SUBLANES
    for m in range(4, gmask_ref.shape[0]):
        odd = [slice(r, r + sb) for r in range(sb, TB, 2 * sb)]
        take = lambda a: jnp.concatenate([a[r] for r in odd], axis=0) if len(odd) > 1 else a[odd[0]]
        t = [_dot(take(inv[h]), lmat[h] * gmask_ref[m]).astype(BF16) for h in heads]
        yield
        new_rows = [take(inv[h]) - _dot(t[h], inv[h]).astype(BF16) for h in heads]
        yield
        rebuilt = []
        for h in heads:
            parts = []
            for n in range(TB // (2 * sb)):
                parts.append(inv[h][2 * n * sb:(2 * n + 1) * sb])
                parts.append(new_rows[h][n * sb:(n + 1) * sb])
            rebuilt.append(jnp.concatenate(parts, axis=0))
        inv = rebuilt
        sb *= 2
    eb = [jnp.exp(bcol[h]) for h in heads]
    s = [s_scr[h] for h in heads]
    zero = jnp.zeros((GDN_D, GDN_D), BF16)
    ks = [None] * GDN_HEADS
    for h0 in range(0, GDN_HEADS, 2):
        h1 = h0 + 1
        lhs = jnp.concatenate([
            jnp.concatenate([kb[h0] * eb[h0], kb[h1] * eb[h1]], axis=1),
            jnp.concatenate([q[h0] * eb[h0], q[h1] * eb[h1]], axis=1)], axis=0)
        s0, s1 = s[h0].astype(BF16), s[h1].astype(BF16)
        rhs = jnp.concatenate([jnp.concatenate([s0, zero], axis=1), jnp.concatenate([zero, s1], axis=1)], axis=0)
        pair = _dot(lhs, rhs)
        ks[h0], ks[h1] = pair[:, :GDN_D], pair[:, GDN_D:]
    yield
    v_new = [_dot(inv[h], v[h] * beta[h] - ks[h][:TB]) for h in heads]
    yield
    o = [ks[h][TB:] + _dot(attn[h], v_new[h]) for h in heads]
    yield
    for h in heads:
        blast = bcol[h][TB - 1:TB, :]
        s_scr[h] = s[h] * jnp.exp(blast) + _dot_tn(k[h] * jnp.exp(blast - bcol[h]), v_new[h])
    o = [o[h] * lax.rsqrt(jnp.mean(o[h] * o[h], axis=-1, keepdims=True) + NORM_EPS) for h in heads]
    return jnp.concatenate(o, axis=1) * normw * _silu(gate)


def _round_robin(gens):
    results = [None] * len(gens)
    live = list(range(len(gens)))
    while live:
        for i in list(live):
            try:
                next(gens[i])
            except StopIteration as stop:
                results[i] = stop.value
                live.remove(i)
    return results


def _mixer_kernel(x_ref, cos_ref, sin_ref, n1w_ref, win_ref, lb_ref, convw_ref, alog_ref, dtb_ref,
                  hgw_ref, gdw_ref, wout_ref, rdecay_ref, rdin_ref, rdout_ref, rwdec_ref, rbd_ref,
                  qmask_ref, vmask_ref, ones64_ref, negmask_ref, gmask_ref,
                  out_ref,
                  proj_scr, sret_scr, shg_scr, sgd_scr, carry_scr):
    @pl.when(pl.program_id(1) == 0)
    def _():
        sret_scr[...] = jnp.zeros_like(sret_scr)
        shg_scr[...] = jnp.zeros_like(shg_scr)
        sgd_scr[...] = jnp.zeros_like(sgd_scr)
        carry_scr[...] = jnp.zeros_like(carry_scr)

    x = x_ref[0]
    h = (x * lax.rsqrt(jnp.mean(x * x, axis=-1, keepdims=True) + NORM_EPS) * n1w_ref[...]).astype(BF16)

    def project(lo, hi):
        proj_scr[:, lo:hi] = jnp.dot(h, win_ref[:, lo:hi], preferred_element_type=F32)
        return proj_scr[:, lo:hi]

    o_gdn, o_hg, o_ret = _round_robin([
        _gdn(project, convw_ref[...], alog_ref[...], dtb_ref[...], gdw_ref[...], negmask_ref,
             gmask_ref, carry_scr, sgd_scr),
        _hgrn2(project, lb_ref[...], hgw_ref[...], ones64_ref, shg_scr),
        _retention(project, cos_ref[0], sin_ref[0], rdecay_ref, rdin_ref, rdout_ref, rwdec_ref,
                   rbd_ref, qmask_ref, vmask_ref, ones64_ref, sret_scr),
    ])
    mixed = jnp.concatenate([o_ret, o_hg, o_gdn], axis=1).astype(BF16)
    out_ref[0] = x + jnp.dot(mixed, wout_ref[...], preferred_element_type=F32)


def _gdn_tables():
    i = np.arange(TB)[:, None]
    j = np.arange(TB)[None, :]
    negmask = np.where(i >= j, 0.0, NEG_BIG).astype(np.float32)
    masks = [i == j, i > j, (i > j) & (i // SUBLANES == j // SUBLANES)]
    s = SUBLANES
    while s < TB:
        masks.append((i // (2 * s) == j // (2 * s)) & (i // s > j // s))
        s *= 2
    return jnp.asarray(negmask), jnp.asarray(np.stack(masks).astype(np.float32), dtype=BF16)


def _retention_tables():
    hh = np.arange(RET_HEADS, dtype=np.float32)
    log_gamma = jnp.log1p(-jnp.exp2(-5.0 - jnp.asarray(hh)))
    idx = jnp.arange(TB, dtype=F32)
    rel = idx[:, None] - idx[None, :]
    causal = rel >= 0
    decay = jnp.where(causal[None], jnp.exp(jnp.where(causal[None], rel[None] * log_gamma[:, None, None], 0.0)), 0.0)
    lane = np.arange(RET_W)
    head_qk = (lane % LANES) // (RET_D // 2)
    head_v = lane // RET_D
    lg_lane = log_gamma[head_qk]
    din = jnp.exp((idx + 1.0)[:, None] * lg_lane[None, :])
    dout = jnp.exp((TB - 1.0 - idx)[:, None] * lg_lane[None, :])
    bd = (head_qk[:, None] == head_v[None, :]).astype(np.float32)
    wdec = jnp.exp(TB * lg_lane)[:, None] * bd
    qmask = np.stack([(head_qk == h) for h in range(RET_HEADS)]).astype(np.float32)[:, None, :]
    vmask = np.stack([(head_v == h) for h in range(RET_HEADS)]).astype(np.float32)[:, None, :]
    ones64 = (head_v[:, None] == head_v[None, :]).astype(np.float32)
    return (decay, din, dout, wdec, jnp.asarray(bd), jnp.asarray(qmask, dtype=BF16),
            jnp.asarray(vmask, dtype=BF16), jnp.asarray(ones64, dtype=BF16))


def _regroup_kernel(w_ref, out_ref):
    w = w_ref[0]
    half = RET_D // 2
    pieces = []
    for base in (0, RET_W):
        for part in range(2):
            pieces += [w[:, base + h * RET_D + part * half:base + h * RET_D + (part + 1) * half]
                       for h in range(RET_HEADS)]
    logits = slice(_GG, _GG + 2 * GDN_HEADS)
    pieces += [w[:, 2 * RET_W:logits.start], w[:, logits.stop:], w[:, logits]]
    pieces.append(jnp.zeros((w.shape[0], D_PROJ_PAD - D_PROJ), F32))
    out_ref[0] = jnp.concatenate(pieces, axis=-1).astype(BF16)


def _regroup_projection(w_in):
    depth, D, width = w_in.shape
    assert width == D_PROJ and D % REGROUP_ROWS == 0
    return pl.pallas_call(
        _regroup_kernel,
        grid=(depth, D // REGROUP_ROWS),
        in_specs=[pl.BlockSpec((1, REGROUP_ROWS, D_PROJ), lambda l, r: (l, r, 0))],
        out_specs=pl.BlockSpec((1, REGROUP_ROWS, D_PROJ_PAD), lambda l, r: (l, r, 0)),
        out_shape=jax.ShapeDtypeStruct((depth, D, D_PROJ_PAD), BF16),
        name="regroup_projection",
    )(w_in)


def _const_spec(shape):
    nd = len(shape)
    return pl.BlockSpec(shape, lambda b, t: (0,) * nd)


def _layer_spec(stacked, layer):
    nd = stacked.ndim
    return pl.BlockSpec((None,) + stacked.shape[1:], lambda b, t: (layer,) + (0,) * (nd - 1))


def _mixer_layer(x, cos_t, sin_t, layer, stacked, tables):
    B, T, D = x.shape
    tok = lambda w: pl.BlockSpec((1, TB, w), lambda b, t: (b, t, 0))
    return pl.pallas_call(
        _mixer_kernel,
        grid=(B, T // TB),
        in_specs=([tok(D), tok(RET_D // 2), tok(RET_D // 2)] + [_layer_spec(a, layer) for a in stacked]
                  + [_const_spec(c.shape) for c in tables]),
        out_specs=tok(D),
        out_shape=jax.ShapeDtypeStruct((B, T, D), F32),
        scratch_shapes=[
            pltpu.VMEM((TB, D_PROJ_PAD), F32),
            pltpu.VMEM((RET_W, RET_W), F32),
            pltpu.VMEM((HG_W, HG_W), F32),
            pltpu.VMEM((GDN_HEADS, GDN_D, GDN_D), F32),
            pltpu.VMEM((SUBLANES, 3 * GDN_W), F32),
        ],
        compiler_params=pltpu.CompilerParams(
            dimension_semantics=("parallel", "arbitrary"), vmem_limit_bytes=VMEM_LIMIT),
        name="mixer_layer",
    )(x, cos_t, sin_t, *stacked, *tables)


def _ffn_kernel(x_ref, n2w_ref, wup_ref, cw_ref, cb_ref, wd_ref, fw_ref, out_ref, carry_scr, act_scr, *,
                final_norm):
    @pl.when(pl.program_id(1) == 0)
    def _():
        carry_scr[...] = jnp.zeros_like(carry_scr)

    x = x_ref[0]
    h = (x * lax.rsqrt(jnp.mean(x * x, axis=-1, keepdims=True) + NORM_EPS) * n2w_ref[...]).astype(BF16)
    for c in range(D_FF // FF_TILE):
        cols = slice(c * FF_TILE, (c + 1) * FF_TILE)
        ys = []
        for off in (0, D_FF):
            wide = slice(off + c * FF_TILE, off + (c + 1) * FF_TILE)
            u = jnp.dot(h, wup_ref[:, wide], preferred_element_type=F32)
            ys.append(_causal_conv(u, carry_scr[:, wide], cw_ref[:, wide], FFN_CONV) + cb_ref[:, wide])
            carry_scr[:, wide] = u[FFN_TB - SUBLANES:FFN_TB]
        act_scr[:, cols] = _silu(ys[0].astype(BF16)) * ys[1].astype(BF16)
    o = x + jnp.dot(act_scr[...], wd_ref[...], preferred_element_type=F32)
    if final_norm:
        o = o * lax.rsqrt(jnp.mean(o * o, axis=-1, keepdims=True) + NORM_EPS) * fw_ref[...]
    out_ref[0] = o


def _ffn_layer(x, layer, n2w, wup, cw, cb, wd, fw, final_norm):
    B, T, D = x.shape
    stacked = [n2w, wup, cw, cb, wd]
    tok = pl.BlockSpec((1, FFN_TB, D), lambda b, t: (b, t, 0))
    return pl.pallas_call(
        functools.partial(_ffn_kernel, final_norm=final_norm),
        grid=(B, T // FFN_TB),
        in_specs=[tok] + [_layer_spec(a, layer) for a in stacked] + [_const_spec(fw.shape)],
        out_specs=tok,
        out_shape=jax.ShapeDtypeStruct((B, T, D), F32),
        scratch_shapes=[pltpu.VMEM((SUBLANES, 2 * D_FF), F32), pltpu.VMEM((FFN_TB, D_FF), BF16)],
        compiler_params=pltpu.CompilerParams(
            dimension_semantics=("parallel", "arbitrary"), vmem_limit_bytes=VMEM_LIMIT),
        name="ffn_layer",
    )(x, *stacked, fw)


def kernel(x, positions, norm1_w, w_in, hg_lb_logits, gdn_conv_w, gdn_A_log, gdn_dt_bias, hg_norm_w,
           gdn_norm_w, w_out, norm2_w, w_up, ffn_conv_w, ffn_conv_b, w_down, final_norm_w):
    depth = w_in.shape[0]
    assert x.shape[1] % TB == 0 and x.shape[1] % FFN_TB == 0 and x.shape[2] == D_MODEL
    p = jax.nn.softmax(hg_lb_logits.astype(F32), axis=0)
    lower_bounds = jnp.maximum(jnp.cumsum(p, axis=0) - p[0:1], 0.0)
    cos_t, sin_t = _rope_tables(positions)
    tables = _retention_tables() + _gdn_tables()
    rows = lambda v: v.astype(F32)[:, None, :]
    pad4 = lambda v: jnp.pad(v.astype(F32), ((0, 0), (0, LANES - GDN_HEADS)))[:, None, :]
    mixer_params = [
        rows(norm1_w), _regroup_projection(w_in), rows(lower_bounds), gdn_conv_w.astype(F32),
        pad4(gdn_A_log), pad4(gdn_dt_bias), rows(jnp.tile(hg_norm_w, (1, HG_HEADS))),
        rows(jnp.tile(gdn_norm_w, (1, GDN_HEADS))), w_out.astype(BF16)]
    ffn_params = [rows(norm2_w), w_up.astype(BF16), ffn_conv_w.astype(F32), rows(ffn_conv_b),
                  w_down.astype(BF16)]
    fw = final_norm_w.astype(F32)[None, :]
    for layer in range(depth):
        x = _mixer_layer(x, cos_t, sin_t, layer, mixer_params, tables)
        x = _ffn_layer(x, layer, *ffn_params, fw, final_norm=(layer == depth - 1))
    return x
```

```python
import functools

import numpy as np
import jax
import jax.numpy as jnp
from jax import lax
from jax.experimental import pallas as pl
from jax.experimental.pallas import tpu as pltpu

F32 = jnp.float32
BF16 = jnp.bfloat16

D_MODEL = 1024
RET_HEADS = 4
RET_D = 64
HG_HEADS = 4
HG_D = 64
GDN_HEADS = 4
GDN_D = 128
GDN_CONV = 4
FFN_CONV = 3
D_FF = 2816
ROPE_BASE = 10000.0
NORM_EPS = 1e-6
EXP_CLIP = 80.0
NEG_BIG = -1e30

LANES = 128
SUBLANES = 8
TB = 256
HG_CHUNK = 16
FF_TILE = 256
FFN_TB = 512
ROPE_ROWS = 512
REGROUP_ROWS = 256
VMEM_LIMIT = 56 * 1024 * 1024

RET_W = RET_HEADS * RET_D
HG_W = HG_HEADS * HG_D
GDN_W = GDN_HEADS * GDN_D
assert RET_W == HG_W == 2 * LANES and RET_D == HG_D

_RQ = 0
_HQ = 4 * RET_W
_GQKV = _HQ + 4 * HG_W
_GG = _GQKV + 3 * GDN_W
_GAB = _GG + GDN_W
D_PROJ = _GAB + 2 * GDN_HEADS
D_PROJ_PAD = -(-D_PROJ // LANES) * LANES


def _dot(a, b):
    return jnp.dot(a.astype(BF16), b.astype(BF16), preferred_element_type=F32)


def _dot_nt(a, b):
    return lax.dot_general(a.astype(BF16), b.astype(BF16), (((1,), (1,)), ((), ())),
                           preferred_element_type=F32)


def _dot_tn(a, b):
    return lax.dot_general(a.astype(BF16), b.astype(BF16), (((0,), (0,)), ((), ())),
                           preferred_element_type=F32)


def _sigmoid(x):
    return 1.0 / (1.0 + jnp.exp(-x))


def _silu(x):
    return x * _sigmoid(x)


def _softplus(x):
    return jnp.maximum(x, 0.0) + jnp.log(1.0 + jnp.exp(-jnp.abs(x)))


def _chunk_cumsum(x, chunk):
    row = lax.broadcasted_iota(jnp.int32, (x.shape[0], 1), 0) % chunk
    s = 1
    while s < chunk:
        x = x + jnp.where(row >= s, pltpu.roll(x, s, axis=0), 0.0)
        s *= 2
    return x


def _causal_conv(u, prev, w, width):
    acc = u * w[width - 1:width]
    row = lax.broadcasted_iota(jnp.int32, (SUBLANES, 1), 0)
    tiles = [prev] + [u[i:i + SUBLANES] for i in range(0, u.shape[0], SUBLANES)]
    for s in range(1, width):
        rot = [pltpu.roll(t, s, axis=0) for t in tiles]
        shifted = jnp.concatenate([jnp.where(row < s, rot[i], rot[i + 1]) for i in range(len(tiles) - 1)], axis=0)
        acc = acc + shifted * w[width - 1 - s:width - s]
    return acc


def _rope_kernel(pos_ref, invf_ref, cos_ref, sin_ref):
    ang = pos_ref[0] * invf_ref[...]
    cos, sin = jnp.cos(ang), jnp.sin(ang)
    rows, half = ang.shape[0], RET_D // 2
    for r in range(LANES // half):
        cos_ref[0, r * rows:(r + 1) * rows, :] = cos[:, r * half:(r + 1) * half]
        sin_ref[0, r * rows:(r + 1) * rows, :] = sin[:, r * half:(r + 1) * half]


def _rope_tables(positions):
    B, T = positions.shape
    half = RET_D // 2
    rep = LANES // half
    inv_freq = ROPE_BASE ** (-jnp.arange(half, dtype=F32) / half)
    invf = jnp.tile(inv_freq, rep)[None, :]
    rows = ROPE_ROWS if T % (rep * ROPE_ROWS) == 0 else T // rep
    pos = positions.astype(F32).reshape(B, T // (rep * rows), rep, rows)
    pos = jnp.repeat(jnp.swapaxes(pos, 2, 3), half, axis=-1).reshape(B, T // rep, LANES)
    table = pl.BlockSpec((1, rep * rows, half), lambda b, t: (b, t, 0))
    return pl.pallas_call(
        _rope_kernel,
        grid=(B, T // (rep * rows)),
        in_specs=[pl.BlockSpec((1, rows, LANES), lambda b, t: (b, t, 0)),
                  pl.BlockSpec((1, LANES), lambda b, t: (0, 0))],
        out_specs=[table, table],
        out_shape=[jax.ShapeDtypeStruct((B, T, half), F32)] * 2,
        name="rope_tables",
    )(pos, invf)


def _retention(project, cos, sin, decay_ref, din_ref, dout_ref, wdec_ref, bd_ref, qmask_ref,
               vmask_ref, ones64_ref, s_scr):
    proj = project(_RQ, _HQ)
    yield
    q, k, v, g = (proj[:, i * RET_W:(i + 1) * RET_W] for i in range(4))

    cos =jnp.concatenate([cos] * RET_HEADS, axis=1)
    sin = jnp.concatenate([sin] * RET_HEADS, axis=1)

    def rot(a):
        a1, a2 = a[:, :LANES], a[:, LANES:]
        return jnp.concatenate([a1 * cos - a2 * sin, a1 * sin + a2 * cos], axis=1)

    qr = rot(q)
    kr = rot(k) * (RET_D ** -0.5)
    qr_b, kr_b, v_b = qr.astype(BF16), kr.astype(BF16), v.astype(BF16)
    yield
    intra = None
    for h in range(RET_HEADS):
        a = _dot_nt(qr_b * qmask_ref[h], kr_b) * decay_ref[h]
        part = _dot(a, v_b * vmask_ref[h])
        intra = part if intra is None else intra + part
        yield
    s = s_scr[...]
    inter = _dot(qr * din_ref[...], s)
    u = _dot_tn(kr * dout_ref[...], v)
    s_scr[...] = wdec_ref[...] * s + bd_ref[...] * u
    yield
    o = intra + inter
    ss = _dot(o * o, ones64_ref[...])
    return o * lax.rsqrt(ss * (1.0 / RET_D) + NORM_EPS) * _silu(g)


def _hgrn2(project, lb, normw, ones64_ref, st_scr):
    proj = project(_HQ, _GQKV)
    yield
    hq, z, vi, gate = (proj[:, i * HG_W:(i + 1) * HG_W] for i in range(4))
    log_sig = jnp.minimum(z, 0.0) - jnp.log(1.0 + jnp.exp(-jnp.abs(z)))
    log_f = log_sig + jnp.log(1.0 + lb * jnp.exp(jnp.minimum(-z, EXP_CLIP)))
    q = _silu(hq)
    k = (1.0 - lb) * _sigmoid(-z)
    b = _chunk_cumsum(log_f, HG_CHUNK)
    ones64 = ones64_ref[...]
    ones64_f = ones64.astype(F32)
    C = HG_CHUNK
    row8 = lax.broadcasted_iota(jnp.int32, (SUBLANES, 1), 0)
    st = st_scr[...]
    outs = []
    yield
    for c in range(TB // C):
        r0 = c * C
        qc, kc, vc, bc = q[r0:r0 + C], k[r0:r0 + C], vi[r0:r0 + C], b[r0:r0 + C]
        blast = bc[C - 1:C, :]
        inter = _dot_nt(qc * jnp.exp(bc), st)
        u = _dot_tn(vc, kc * jnp.exp(blast - bc))
        st = jnp.exp(blast) * st + ones64_f * u
        parts = []
        for tile in range(C // SUBLANES):
            lo = tile * SUBLANES
            qt, bt = qc[lo:lo + SUBLANES], bc[lo:lo + SUBLANES]
            for j in range(lo + SUBLANES):
                diff = bt - bc[j:j + 1, :]
                if j > lo:
                    diff = jnp.where(row8 >= j - lo, diff, NEG_BIG)
                parts.append(qt * kc[j:j + 1, :] * jnp.exp(diff))
        p = jnp.concatenate(parts, axis=0).astype(BF16)
        r = jnp.dot(p, ones64, preferred_element_type=F32)
        accs = []
        off = 0
        for tile in range(C // SUBLANES):
            lo = tile * SUBLANES
            acc = inter[lo:lo + SUBLANES]
            for j in range(lo + SUBLANES):
                acc = acc + r[off:off + SUBLANES] * vc[j:j + 1, :]
                off += SUBLANES
            accs.append(acc)
        outs.extend(accs)
        yield
    st_scr[...] = st
    o = jnp.concatenate(outs, axis=0)
    ss = _dot(o * o, ones64)
    return o * lax.rsqrt(ss * (1.0 / HG_D) + NORM_EPS) * normw * _silu(gate)


def _gdn(project, convw, alog, dtb, normw, negmask_ref, gmask_ref, carry_scr, s_scr):
    proj = project(_GQKV, D_PROJ_PAD)
    yield
    raw = proj[:, 0:3 * GDN_W]
    qkv = _silu(_causal_conv(raw, carry_scr[...], convw, GDN_CONV))
    carry_scr[...] = raw[TB - SUBLANES:TB]
    W = GDN_W
    gab = proj[:, _GAB - _GQKV:_GAB - _GQKV + LANES]
    gate = proj[:, _GG - _GQKV:_GG - _GQKV + W]
    yield
    g_all = -jnp.exp(alog) * _softplus(gab + dtb)
    beta_all = _sigmoid(gab)
    b_all = _chunk_cumsum(g_all, TB)
    b_rows = jnp.transpose(b_all)
    negmask = negmask_ref[...]
    eye = gmask_ref[0].astype(F32)
    heads = range(GDN_HEADS)

    def l2n(a):
        return a * lax.rsqrt(jnp.sum(a * a, axis=-1, keepdims=True) + NORM_EPS)

    q = [l2n(qkv[:, h * GDN_D:(h + 1) * GDN_D]) * (GDN_D ** -0.5) for h in heads]
    k = [l2n(qkv[:, W + h * GDN_D:W + (h + 1) * GDN_D]) for h in heads]
    v = [qkv[:, 2 * W + h * GDN_D:2 * W + (h + 1) * GDN_D] for h in heads]
    bcol = [b_all[:, h:h + 1] for h in heads]
    beta = [beta_all[:, GDN_HEADS + h:GDN_HEADS + h + 1] for h in heads]
    gam = [jnp.exp(bcol[h] - b_rows[h:h + 1, :] + negmask) for h in heads]
    kb = [k[h] * beta[h] for h in heads]
    yield
    kq = [_dot_nt(jnp.concatenate([kb[h], q[h]], axis=0), k[h]) for h in heads]
    yield
    lmat = [(kq[h][:TB] * gam[h]).astype(BF16) * gmask_ref[1] for h in heads]
    attn = [(kq[h][TB:] * gam[h]).astype(BF16) for h in heads]
    l8 = [lmat[h] * gmask_ref[2] for h in heads]
    yield
    p2 = [_dot(l8[h], l8[h]).astype(BF16) for h in heads]
    p4 = [_dot(p2[h], p2[h]).astype(BF16) for h in heads]
    yield
    x0 = [eye - l8[h].astype(F32) for h in heads]
    x1 = [x0[h] + _dot(x0[h], p2[h]) for h in heads]
    yield
    inv = [(x1[h] + _dot(x1[h], p4[h])).astype(BF16) for h in heads]
    yield
    t = [_dot(inv[h], lmat[h] * gmask_ref[3]).astype(BF16) for h in heads]
    yield
    inv = [inv[h] - _dot(t[h], inv[h]).astype(BF16) for h in heads]
    yield
    sb = 2 * SUBLANES
    for m in range(4, gmask_ref.shape[0]):
        odd = [slice(r, r + sb) for r in range(sb, TB, 2 * sb)]
        take = lambda a: jnp.concatenate([a[r] for r in odd], axis=0) if len(odd) > 1 else a[odd[0]]
        t = [_dot(take(inv[h]), lmat[h] * gmask_ref[m]).astype(BF16) for h in heads]
        yield
        new_rows = [take(inv[h]) - _dot(t[h], inv[h]).astype(BF16) for h in heads]
        yield
        rebuilt = []
        for h in heads:
            parts = []
            for n in range(TB // (2 * sb)):
                parts.append(inv[h][2 * n * sb:(2 * n + 1) * sb])
                parts.append(new_rows[h][n * sb:(n + 1) * sb])
            rebuilt.append(jnp.concatenate(parts, axis=0))
        inv = rebuilt
        sb *= 2
    eb = [jnp.exp(bcol[h]) for h in heads]
    s = [s_scr[h] for h in heads]
    zero = jnp.zeros((GDN_D, GDN_D), BF16)
    ks = [None] * GDN_HEADS
    for h0 in range(0, GDN_HEADS, 2):
        h1 = h0 + 1
        lhs = jnp.concatenate([
            jnp.concatenate([kb[h0] * eb[h0], kb[h1] * eb[h1]], axis=1),
            jnp.concatenate([q[h0] * eb[h0], q[h1] * eb[h1]], axis=1)], axis=0)
        s0, s1 = s[h0].astype(BF16), s[h1].astype(BF16)
        rhs = jnp.concatenate([jnp.concatenate([s0, zero], axis=1), jnp.concatenate([zero, s1], axis=1)], axis=0)
        pair = _dot(lhs, rhs)
        ks[h0], ks[h1] = pair[:, :GDN_D], pair[:, GDN_D:]
    yield
    v_new = [_dot(inv[h], v[h] * beta[h] - ks[h][:TB]) for h in heads]
    yield
    o = [ks[h][TB:] + _dot(attn[h], v_new[h]) for h in heads]
    yield
    for h in heads:
        blast = bcol[h][TB - 1:TB, :]
        s_scr[h] = s[h] * jnp.exp(blast) + _dot_tn(k[h] * jnp.exp(blast - bcol[h]), v_new[h])
    o = [o[h] * lax.rsqrt(jnp.mean(o[h] * o[h], axis=-1, keepdims=True) + NORM_EPS) for h in heads]
    return jnp.concatenate(o, axis=1) * normw * _silu(gate)


def _round_robin(gens):
    results = [None] * len(gens)
    live = list(range(len(gens)))
    while live:
        for i in list(live):
            try:
                next(gens[i])
            except StopIteration as stop:
                results[i] = stop.value
                live.remove(i)
    return results


def _mixer_kernel(x_ref, cos_ref, sin_ref, n1w_ref, win_ref, lb_ref, convw_ref, alog_ref, dtb_ref,
                  hgw_ref, gdw_ref, wout_ref, rdecay_ref, rdin_ref, rdout_ref, rwdec_ref, rbd_ref,
                  qmask_ref, vmask_ref, ones64_ref, negmask_ref, gmask_ref,
                  out_ref,
                  proj_scr, sret_scr, shg_scr, sgd_scr, carry_scr):
    @pl.when(pl.program_id(1) == 0)
    def _():
        sret_scr[...] = jnp.zeros_like(sret_scr)
        shg_scr[...] = jnp.zeros_like(shg_scr)
        sgd_scr[...] = jnp.zeros_like(sgd_scr)
        carry_scr[...] = jnp.zeros_like(carry_scr)

    x = x_ref[0]
    h = (x * lax.rsqrt(jnp.mean(x * x, axis=-1, keepdims=True) + NORM_EPS) * n1w_ref[...]).astype(BF16)

    def project(lo, hi):
        proj_scr[:, lo:hi] = jnp.dot(h, win_ref[:, lo:hi], preferred_element_type=F32)
        return proj_scr[:, lo:hi]

    o_gdn, o_hg, o_ret = _round_robin([
        _gdn(project, convw_ref[...], alog_ref[...], dtb_ref[...], gdw_ref[...], negmask_ref,
             gmask_ref, carry_scr, sgd_scr),
        _hgrn2(project, lb_ref[...], hgw_ref[...], ones64_ref, shg_scr),
        _retention(project, cos_ref[0], sin_ref[0], rdecay_ref, rdin_ref, rdout_ref, rwdec_ref,
                   rbd_ref, qmask_ref, vmask_ref, ones64_ref, sret_scr),
    ])
    mixed = jnp.concatenate([o_ret, o_hg, o_gdn], axis=1).astype(BF16)
    out_ref[0] = x + jnp.dot(mixed, wout_ref[...], preferred_element_type=F32)


def _gdn_tables():
    i = np.arange(TB)[:, None]
    j = np.arange(TB)[None, :]
    negmask = np.where(i >= j, 0.0, NEG_BIG).astype(np.float32)
    masks = [i == j, i > j, (i > j) & (i // SUBLANES == j // SUBLANES)]
    s = SUBLANES
    while s < TB:
        masks.append((i // (2 * s) == j // (2 * s)) & (i // s > j // s))
        s *= 2
    return jnp.asarray(negmask), jnp.asarray(np.stack(masks).astype(np.float32), dtype=BF16)


def _retention_tables():
    hh = np.arange(RET_HEADS, dtype=np.float32)
    log_gamma = jnp.log1p(-jnp.exp2(-5.0 - jnp.asarray(hh)))
    idx = jnp.arange(TB, dtype=F32)
    rel = idx[:, None] - idx[None, :]
    causal = rel >= 0
    decay = jnp.where(causal[None], jnp.exp(jnp.where(causal[None], rel[None] * log_gamma[:, None, None], 0.0)), 0.0)
    lane = np.arange(RET_W)
    head_qk = (lane % LANES) // (RET_D // 2)
    head_v = lane // RET_D
    lg_lane = log_gamma[head_qk]
    din = jnp.exp((idx + 1.0)[:, None] * lg_lane[None, :])
    dout = jnp.exp((TB - 1.0 - idx)[:, None] * lg_lane[None, :])
    bd = (head_qk[:, None] == head_v[None, :]).astype(np.float32)
    wdec = jnp.exp(TB * lg_lane)[:, None] * bd
    qmask = np.stack([(head_qk == h) for h in range(RET_HEADS)]).astype(np.float32)[:, None, :]
    vmask = np.stack([(head_v == h) for h in range(RET_HEADS)]).astype(np.float32)[:, None, :]
    ones64 = (head_v[:, None] == head_v[None, :]).astype(np.float32)
    return (decay, din, dout, wdec, jnp.asarray(bd), jnp.asarray(qmask, dtype=BF16),
            jnp.asarray(vmask, dtype=BF16), jnp.asarray(ones64, dtype=BF16))


def _regroup_kernel(w_ref, out_ref):
    w = w_ref[0]
    half = RET_D // 2
    pieces = []
    for base in (0, RET_W):
        for part in range(2):
            pieces += [w[:, base + h * RET_D + part * half:base + h * RET_D + (part + 1) * half]
                       for h in range(RET_HEADS)]
    logits = slice(_GG, _GG + 2 * GDN_HEADS)
    pieces += [w[:, 2 * RET_W:logits.start], w[:, logits.stop:], w[:, logits]]
    pieces.append(jnp.zeros((w.shape[0], D_PROJ_PAD - D_PROJ), w.dtype))
    out_ref[0] = jnp.concatenate(pieces, axis=-1).astype(BF16)


def _regroup_projection(w_in):
    depth, D, width = w_in.shape
    assert width == D_PROJ and D % REGROUP_ROWS == 0
    return pl.pallas_call(
        _regroup_kernel,
        grid=(depth, D // REGROUP_ROWS),
        in_specs=[pl.BlockSpec((1, REGROUP_ROWS, D_PROJ), lambda l, r: (l, r, 0))],
        out_specs=pl.BlockSpec((1, REGROUP_ROWS, D_PROJ_PAD), lambda l, r: (l, r, 0)),
        out_shape=jax.ShapeDtypeStruct((depth, D, D_PROJ_PAD), BF16),
        name="regroup_projection",
    )(w_in.astype(BF16))


def _const_spec(shape):
    nd = len(shape)
    return pl.BlockSpec(shape, lambda b, t: (0,) * nd)


def _layer_spec(stacked, layer):
    nd = stacked.ndim
    return pl.BlockSpec((None,) + stacked.shape[1:], lambda b, t: (layer,) + (0,) * (nd - 1))


def _mixer_layer(x, cos_t, sin_t, layer, stacked, tables):
    B, T, D = x.shape
    tok = lambda w: pl.BlockSpec((1, TB, w), lambda b, t: (b, t, 0))
    return pl.pallas_call(
        _mixer_kernel,
        grid=(B, T // TB),
        in_specs=([tok(D), tok(RET_D // 2), tok(RET_D // 2)] + [_layer_spec(a, layer) for a in stacked]
                  + [_const_spec(c.shape) for c in tables]),
        out_specs=tok(D),
        out_shape=jax.ShapeDtypeStruct((B, T, D), F32),
        scratch_shapes=[
            pltpu.VMEM((TB, D_PROJ_PAD), F32),
            pltpu.VMEM((RET_W, RET_W), F32),
            pltpu.VMEM((HG_W, HG_W), F32),
            pltpu.VMEM((GDN_HEADS, GDN_D, GDN_D), F32),
            pltpu.VMEM((SUBLANES, 3 * GDN_W), F32),
        ],
        compiler_params=pltpu.CompilerParams(
            dimension_semantics=("parallel", "arbitrary"), vmem_limit_bytes=VMEM_LIMIT),
        name="mixer_layer",
    )(x, cos_t, sin_t, *stacked, *tables)


def _ffn_kernel(x_ref, n2w_ref, wup_ref, cw_ref, cb_ref, wd_ref, fw_ref, out_ref, carry_scr, act_scr, *,
                final_norm):
    @pl.when(pl.program_id(1) == 0)
    def _():
        carry_scr[...] = jnp.zeros_like(carry_scr)

    x = x_ref[0]
    h = (x * lax.rsqrt(jnp.mean(x * x, axis=-1, keepdims=True) + NORM_EPS) * n2w_ref[...]).astype(BF16)
    for c in range(D_FF // FF_TILE):
        cols = slice(c * FF_TILE, (c + 1) * FF_TILE)
        ys = []
        for off in (0, D_FF):
            wide = slice(off + c * FF_TILE, off + (c + 1) * FF_TILE)
            u = jnp.dot(h, wup_ref[:, wide], preferred_element_type=F32)
            ys.append(_causal_conv(u, carry_scr[:, wide], cw_ref[:, wide], FFN_CONV) + cb_ref[:, wide])
            carry_scr[:, wide] = u[FFN_TB - SUBLANES:FFN_TB]
        act_scr[:, cols] = _silu(ys[0].astype(BF16)) * ys[1].astype(BF16)
    o = x + jnp.dot(act_scr[...], wd_ref[...], preferred_element_type=F32)
    if final_norm:
        o = o * lax.rsqrt(jnp.mean(o * o, axis=-1, keepdims=True) + NORM_EPS) * fw_ref[...]
    out_ref[0] = o


def _ffn_layer(x, layer, n2w, wup, cw, cb, wd, fw, final_norm):
    B, T, D = x.shape
    stacked = [n2w, wup, cw, cb, wd]
    tok = pl.BlockSpec((1, FFN_TB, D), lambda b, t: (b, t, 0))
    return pl.pallas_call(
        functools.partial(_ffn_kernel, final_norm=final_norm),
        grid=(B, T // FFN_TB),
        in_specs=[tok] + [_layer_spec(a, layer) for a in stacked] + [_const_spec(fw.shape)],
        out_specs=tok,
        out_shape=jax.ShapeDtypeStruct((B, T, D), F32),
        scratch_shapes=[pltpu.VMEM((SUBLANES, 2 * D_FF), F32), pltpu.VMEM((FFN_TB, D_FF), BF16)],
        compiler_params=pltpu.CompilerParams(
            dimension_semantics=("parallel", "arbitrary"), vmem_limit_bytes=VMEM_LIMIT),
        name="ffn_layer",
    )(x, *stacked, fw)


def kernel(x, positions, norm1_w, w_in, hg_lb_logits, gdn_conv_w, gdn_A_log, gdn_dt_bias, hg_norm_w,
           gdn_norm_w, w_out, norm2_w, w_up, ffn_conv_w, ffn_conv_b, w_down, final_norm_w):
    depth = w_in.shape[0]
    assert x.shape[1] % TB == 0 and x.shape[1] % FFN_TB == 0 and x.shape[2] == D_MODEL
    p = jax.nn.softmax(hg_lb_logits.astype(F32), axis=0)
    lower_bounds = jnp.maximum(jnp.cumsum(p, axis=0) - p[0:1], 0.0)
    cos_t, sin_t = _rope_tables(positions)
    tables = _retention_tables() + _gdn_tables()
    rows = lambda v: v.astype(F32)[:, None, :]
    pad4 = lambda v: jnp.pad(v.astype(F32), ((0, 0), (0, LANES - GDN_HEADS)))[:, None, :]
    mixer_params = [
        rows(norm1_w), _regroup_projection(w_in), rows(lower_bounds), gdn_conv_w.astype(F32),
        pad4(gdn_A_log), pad4(gdn_dt_bias), rows(jnp.tile(hg_norm_w, (1, HG_HEADS))),
        rows(jnp.tile(gdn_norm_w, (1, GDN_HEADS))), w_out.astype(BF16)]
    ffn_params = [rows(norm2_w), w_up.astype(BF16), ffn_conv_w.astype(F32), rows(ffn_conv_b),
                  w_down.astype(BF16)]
    fw = final_norm_w.astype(F32)[None, :]
    for layer in range(depth):
        x = _mixer_layer(x, cos_t, sin_t, layer, mixer_params, tables)
        x = _ffn_layer(x, layer, *ffn_params, fw, final_norm=(layer == depth - 1))
    return x
```
